```python
import jax, jax.numpy as jnp
from jax import lax
import numpy as np

D_MODEL = 1024
BATCH = 8
SEQ = 2048
DEPTH = 2
DEC_BATCH = 128
DEC_SEQ = 8
PAST_LEN = 16384
PAGE_SIZE = 128

N_A_LAYERS = DEPTH // 2
N_B_LAYERS = DEPTH - N_A_LAYERS
D_RNN = D_MODEL
RG_BLOCKS = 8
RG_BLOCK_W = D_RNN // RG_BLOCKS
CONV_W = 4
RG_C = 8.0
HEAD_DIM = 64
N_HEADS = D_MODEL // HEAD_DIM
N_KV_HEADS = 4
GROUP = N_HEADS // N_KV_HEADS
WINDOW = 128
ROPE_THETA = 10000.0
N_KEYS = 128
N_EXPERTS = N_KEYS * N_KEYS
PEER_HEADS = 8
PEER_TOPK = 16
D_KEY = 256
PEER_CHUNK = 128
EPS = 1e-6

kernel_name = 'yoco_rglru_swa_sink_peer_step'


def rmsnorm(x, g):
    xf = x.astype(jnp.float32)
    y = xf * lax.rsqrt(jnp.mean(xf * xf, axis=-1, keepdims=True) + EPS)
    return (y * g.astype(jnp.float32)).astype(x.dtype)


def rope(x, pos):
    half = HEAD_DIM // 2
    inv = ROPE_THETA ** (-jnp.arange(half, dtype=jnp.float32) / half)
    ang = pos.astype(jnp.float32)[:, None] * inv[None, :]
    cos = jnp.cos(ang)[:, None, :]
    sin = jnp.sin(ang)[:, None, :]
    xf = x.astype(jnp.float32)
    x1, x2 = xf[..., :half], xf[..., half:]
    return jnp.concatenate([x1 * cos - x2 * sin, x2 * cos + x1 * sin], axis=-1).astype(x.dtype)


def causal_conv(x, buf, w, b):
    xp = jnp.concatenate([buf.astype(x.dtype), x], axis=1)
    t = x.shape[1]
    y = b
    for k in range(CONV_W):
        y = y + w[k] * xp[:, k:k + t]
    return y, xp[:, -(CONV_W - 1):]


def block_diag(x, w, b):
    xb = x.reshape(x.shape[:-1] + (RG_BLOCKS, RG_BLOCK_W))
    return jnp.einsum('btnc,ncd->btnd', xb, w).reshape(x.shape) + b


def rg_lru(x, h0, wa, ba, wx, bx, lam):
    f32 = jnp.float32
    r = jax.nn.sigmoid(block_diag(x, wa, ba).astype(f32))
    i = jax.nn.sigmoid(block_diag(x, wx, bx).astype(f32))
    log_a = -RG_C * r * jax.nn.softplus(-lam.astype(f32))
    a = jnp.exp(log_a)
    mult = jnp.sqrt(-jnp.expm1(2.0 * log_a))
    bterm = mult * i * x.astype(f32)
    bterm = bterm.at[:, 0].add(a[:, 0] * h0.astype(f32))

    def comb(lhs, rhs):
        a1, b1 = lhs
        a2, b2 = rhs
        return a1 * a2, a2 * b1 + b2

    _, h = lax.associative_scan(comb, (a, bterm), axis=1)
    return h.astype(x.dtype), h[:, -1].astype(x.dtype)


def recurrent_block(xn, conv_buf, h0, w_in, conv_w, conv_b, wa, ba, wx, bx, lam, w_out):
    proj = xn @ w_in
    gate = jax.nn.gelu(proj[..., :D_RNN], approximate=True)
    xc, new_buf = causal_conv(proj[..., D_RNN:], conv_buf, conv_w, conv_b)
    h, h_last = rg_lru(xc, h0, wa, ba, wx, bx, lam)
    return (h * gate) @ w_out, new_buf, h_last


def sink_softmax(s, mask, sink):
    s = jnp.where(mask, s, -jnp.inf)
    m = jnp.maximum(jnp.max(s, axis=-1, keepdims=True), sink)
    p = jnp.exp(s - m)
    return p / (jnp.sum(p, axis=-1, keepdims=True) + jnp.exp(sink - m))


def swa_prompt(q, k, v, sinks):
    b, t = q.shape[:2]
    c = WINDOW
    nb = t // c
    scale = HEAD_DIM ** -0.5
    qb = q.reshape(b, nb, c, N_KV_HEADS, GROUP, HEAD_DIM)
    pad = jnp.zeros((b, c, N_KV_HEADS, HEAD_DIM), k.dtype)
    kblk = jnp.concatenate([pad, k], axis=1).reshape(b, nb + 1, c, N_KV_HEADS, HEAD_DIM)
    vblk = jnp.concatenate([pad, v], axis=1).reshape(b, nb + 1, c, N_KV_HEADS, HEAD_DIM)
    kb = jnp.concatenate([kblk[:, :-1], kblk[:, 1:]], axis=2)
    vb = jnp.concatenate([vblk[:, :-1], vblk[:, 1:]], axis=2)
    blk = jnp.arange(nb)[:, None, None]
    qpos = blk * c + jnp.arange(c)[None, :, None]
    kpos = (blk - 1) * c + jnp.arange(2 * c)[None, None, :]
    mask = (kpos >= 0) & (kpos <= qpos) & (qpos - kpos < WINDOW)
    s = jnp.einsum('bnqkgd,bnskd->bnkgqs', qb, kb).astype(jnp.float32) * scale
    sink = sinks.astype(jnp.float32).reshape(N_KV_HEADS, GROUP, 1, 1)
    p = sink_softmax(s, mask[None, :, None, None], sink)
    o = jnp.einsum('bnkgqs,bnskd->bnqkgd', p.astype(v.dtype), vb)
    return o.reshape(b, t, N_HEADS * HEAD_DIM)


def swa_sample(q, kall, vall, qpos, kpos, sinks):
    b, nq = q.shape[:2]
    scale = HEAD_DIM ** -0.5
    qg = q.reshape(b, nq, N_KV_HEADS, GROUP, HEAD_DIM)
    mask = (kpos[None, :] <= qpos[:, None]) & (qpos[:, None] - kpos[None, :] < WINDOW)
    s = jnp.einsum('bqkgd,bskd->bkgqs', qg, kall).astype(jnp.float32) * scale
    sink = sinks.astype(jnp.float32).reshape(N_KV_HEADS, GROUP, 1, 1)
    p = sink_softmax(s, mask, sink)
    o = jnp.einsum('bkgqs,bskd->bqkgd', p.astype(vall.dtype), vall)
    return o.reshape(b, nq, N_HEADS * HEAD_DIM)


def peer(xn, wq, subkeys, u, v):
    shp = xn.shape
    xt = xn.reshape(-1, D_MODEL)
    t = xt.shape[0]
    q = (xt @ wq).reshape(t, PEER_HEADS, 2, D_KEY // 2)
    s = jnp.einsum('thpc,pnc->thpn', q, subkeys).astype(jnp.float32)
    sv, si = lax.top_k(s, PEER_TOPK)
    cand = sv[:, :, 0, :, None] + sv[:, :, 1, None, :]
    cv, ci = lax.top_k(cand.reshape(t, PEER_HEADS, PEER_TOPK * PEER_TOPK), PEER_TOPK)
    ia = jnp.take_along_axis(si[:, :, 0], ci // PEER_TOPK, axis=-1)
    ib = jnp.take_along_axis(si[:, :, 1], ci % PEER_TOPK, axis=-1)
    eidx = ia * N_KEYS + ib
    gate = jax.nn.softmax(cv, axis=-1).astype(xt.dtype)
    npad = (-t) % PEER_CHUNK
    nc = (t + npad) // PEER_CHUNK
    xp = jnp.pad(xt, ((0, npad), (0, 0))).reshape(nc, PEER_CHUNK, D_MODEL)
    ep = jnp.pad(eidx, ((0, npad), (0, 0), (0, 0))).reshape(nc, PEER_CHUNK, PEER_HEADS, PEER_TOPK)
    gp = jnp.pad(gate, ((0, npad), (0, 0), (0, 0))).reshape(nc, PEER_CHUNK, PEER_HEADS, PEER_TOPK)

    def chunk(args):
        xc, ec, gc = args
        hc = jax.nn.gelu(jnp.einsum('chkd,cd->chk', u[ec], xc), approximate=False)
        return jnp.einsum('chk,chkd->cd', hc * gc, v[ec])

    out = lax.map(chunk, (xp, ep, gp))
    return out.reshape(-1, D_MODEL)[:t].reshape(shp)


def trunk(x, pos, is_prompt, conv_states, h_states, kbuf, vbuf, w):
    b, t = x.shape[:2]
    new_conv, new_h = [], []
    k_new = v_new = None
    kall = vall = kpos = None
    for layer in range(DEPTH):
        if layer < N_A_LAYERS:
            a = layer
            if is_prompt:
                cb = jnp.zeros((b, CONV_W - 1, D_RNN), x.dtype)
                h0 = jnp.zeros((b, D_RNN), x.dtype)
            else:
                cb = conv_states[a]
                h0 = h_states[a]
            y, cb_new, h_last = recurrent_block(
                rmsnorm(x, w['rg_norm'][a]), cb, h0, w['rg_w_in'][a], w['rg_conv_w'][a],
                w['rg_conv_b'][a], w['rg_wa'][a], w['rg_ba'][a], w['rg_wx'][a], w['rg_bx'][a],
                w['rg_lambda'][a], w['rg_w_out'][a])
            x = x + y
            new_conv.append(cb_new)
            new_h.append(h_last)
        else:
            bi = layer - N_A_LAYERS
            if bi == 0:
                kv = (rmsnorm(x, w['kv_norm']) @ w['w_kv']).reshape(b, t, 2, N_KV_HEADS, HEAD_DIM)
                k = rope(kv[:, :, 0], pos)
                v = kv[:, :, 1]
                if is_prompt:
                    k_new, v_new = k[:, -WINDOW:], v[:, -WINDOW:]
                else:
                    kall = jnp.concatenate([kbuf.astype(k.dtype), k], axis=1)
                    vall = jnp.concatenate([vbuf.astype(v.dtype), v], axis=1)
                    kpos = jnp.concatenate([PAST_LEN - WINDOW + jnp.arange(WINDOW), pos])
                    k_new, v_new = kall[:, -WINDOW:], vall[:, -WINDOW:]
            q = rope((rmsnorm(x, w['attn_norm'][bi]) @ w['w_q'][bi]).reshape(b, t, N_HEADS, HEAD_DIM), pos)
            if is_prompt:
                o = swa_prompt(q, k, v, w['sinks'][bi])
            else:
                o = swa_sample(q, kall, vall, pos, kpos, w['sinks'][bi])
            x = x + o @ w['w_o'][bi]
        x = x + peer(rmsnorm(x, w['ffn_norm'][layer]), w['peer_wq'][layer], w['peer_subkeys'][layer],
                     w['peer_u'][layer], w['peer_v'][layer])
    return rmsnorm(x, w['final_norm']), jnp.stack(new_conv), jnp.stack(new_h), k_new, v_new


def setup_inputs(seed: int = 0) -> dict:
    key = jax.random.key(seed)
    ks = jax.random.split(key, 32)
    f32 = jnp.float32

    def nrm(k, shape, scale):
        return jax.random.normal(k, shape, f32) * scale

    def gain(k, shape):
        return 1.0 + 0.02 * jax.random.normal(k, shape, f32)

    na, nb = N_A_LAYERS, N_B_LAYERS
    a0 = jax.random.uniform(ks[10], (na, D_RNN), f32, 0.9, 0.999)
    return {
        'x_prompt': nrm(ks[0], (BATCH, SEQ, D_MODEL), 1.0),
        'x_sample': nrm(ks[1], (DEC_BATCH, DEC_SEQ, D_MODEL), 1.0),
        'state_conv': nrm(ks[2], (na, DEC_BATCH, CONV_W - 1, D_RNN), 1.0),
        'state_h': nrm(ks[3], (na, DEC_BATCH, D_RNN), 0.5),
        'cache_k': nrm(ks[4], (DEC_BATCH, WINDOW, N_KV_HEADS, HEAD_DIM), 1.0),
        'cache_v': nrm(ks[5], (DEC_BATCH, WINDOW, N_KV_HEADS, HEAD_DIM), 1.0),
        'rg_norm': gain(ks[6], (na, D_MODEL)),
        'rg_w_in': nrm(ks[7], (na, D_MODEL, 2 * D_RNN), D_MODEL ** -0.5),
        'rg_conv_w': nrm(ks[8], (na, CONV_W, D_RNN), CONV_W ** -0.5),
        'rg_conv_b': nrm(ks[9], (na, D_RNN), 0.02),
        'rg_wa': nrm(ks[11], (na, RG_BLOCKS, RG_BLOCK_W, RG_BLOCK_W), RG_BLOCK_W ** -0.5),
        'rg_ba': nrm(ks[12], (na, D_RNN), 0.02),
        'rg_wx': nrm(ks[13], (na, RG_BLOCKS, RG_BLOCK_W, RG_BLOCK_W), RG_BLOCK_W ** -0.5),
        'rg_bx': nrm(ks[14], (na, D_RNN), 0.02),
        'rg_lambda': jnp.log(a0) - jnp.log1p(-a0),
        'rg_w_out': nrm(ks[15], (na, D_RNN, D_MODEL), D_RNN ** -0.5),
        'kv_norm': gain(ks[16], (D_MODEL,)),
        'w_kv': nrm(ks[17], (D_MODEL, 2 * N_KV_HEADS * HEAD_DIM), D_MODEL ** -0.5),
        'attn_norm': gain(ks[18], (nb, D_MODEL)),
        'w_q': nrm(ks[19], (nb, D_MODEL, N_HEADS * HEAD_DIM), D_MODEL ** -0.5),
        'sinks': nrm(ks[20], (nb, N_HEADS), 1.0),
        'w_o': nrm(ks[21], (nb, N_HEADS * HEAD_DIM, D_MODEL), (N_HEADS * HEAD_DIM) ** -0.5),
        'ffn_norm': gain(ks[22], (DEPTH, D_MODEL)),
        'peer_wq': nrm(ks[23], (DEPTH, D_MODEL, PEER_HEADS * D_KEY), D_MODEL ** -0.5),
        'peer_subkeys': nrm(ks[24], (DEPTH, 2, N_KEYS, D_KEY // 2), (D_KEY // 2) ** -0.5),
        'peer_u': nrm(ks[25], (DEPTH, N_EXPERTS, D_MODEL), D_MODEL ** -0.5),
        'peer_v': nrm(ks[26], (DEPTH, N_EXPERTS, D_MODEL), 0.1),
        'final_norm': gain(ks[27], (D_MODEL,)),
    }


def reference(x_prompt, x_sample, state_conv, state_h, cache_k, cache_v,
              rg_norm, rg_w_in, rg_conv_w, rg_conv_b, rg_wa, rg_ba, rg_wx, rg_bx, rg_lambda, rg_w_out,
              kv_norm, w_kv, attn_norm, w_q, sinks, w_o,
              ffn_norm, peer_wq, peer_subkeys, peer_u, peer_v, final_norm):
    w = dict(rg_norm=rg_norm, rg_w_in=rg_w_in, rg_conv_w=rg_conv_w, rg_conv_b=rg_conv_b,
             rg_wa=rg_wa, rg_ba=rg_ba, rg_wx=rg_wx, rg_bx=rg_bx, rg_lambda=rg_lambda,
             rg_w_out=rg_w_out, kv_norm=kv_norm, w_kv=w_kv, attn_norm=attn_norm, w_q=w_q,
             sinks=sinks, w_o=w_o, ffn_norm=ffn_norm, peer_wq=peer_wq,
             peer_subkeys=peer_subkeys, peer_u=peer_u, peer_v=peer_v, final_norm=final_norm)
    pos_p = jnp.arange(x_prompt.shape[1])
    y_prompt, conv_p, h_p, k_p, v_p = trunk(x_prompt, pos_p, True, None, None, None, None, w)
    pos_s = PAST_LEN + jnp.arange(x_sample.shape[1])
    y_sample, conv_s, h_s, k_s, v_s = trunk(x_sample, pos_s, False, state_conv, state_h,
                                            cache_k, cache_v, w)
    return (y_prompt, y_sample, conv_p, h_p, k_p, v_p, conv_s, h_s, k_s, v_s)
```

```python
import functools
import math

import jax
import jax.numpy as jnp
from jax import lax
from jax.experimental import pallas as pl
from jax.experimental.pallas import tpu as pltpu

D_MODEL = 1024
PAST_LEN = 16384
D_RNN = D_MODEL
RG_BLOCKS = 8
RG_BLOCK_W = D_RNN // RG_BLOCKS
CONV_W = 4
RG_C = 8.0
HEAD_DIM = 64
N_HEADS = D_MODEL // HEAD_DIM
N_KV_HEADS = 4
GROUP = N_HEADS // N_KV_HEADS
WINDOW = 128
ROPE_THETA = 10000.0
N_KEYS = 128
N_EXPERTS = N_KEYS * N_KEYS
PEER_HEADS = 8
PEER_TOPK = 16
D_KEY = 256
EPS = 1e-6

KV_DIM = N_KV_HEADS * HEAD_DIM
Q_DIM = N_HEADS * HEAD_DIM
PQ_DIM = PEER_HEADS * D_KEY

V7X_LANES = 128
V7X_SUBLANES = 8
V7X_VMEM_LIMIT_BYTES = 56 * 1024 * 1024

BF16 = jnp.bfloat16
F32 = jnp.float32


def _cparams(*sem):
    return pltpu.CompilerParams(dimension_semantics=sem, vmem_limit_bytes=V7X_VMEM_LIMIT_BYTES)


def _rms(x, g):
    return x * lax.rsqrt(jnp.mean(x * x, axis=-1, keepdims=True) + EPS) * g


def _gelu_tanh(x):
    return 0.5 * x * (1.0 + jnp.tanh(math.sqrt(2.0 / math.pi) * (x + 0.044715 * (x * x * x))))


def _gelu_erf(x):
    return 0.5 * x * (1.0 + lax.erf(x * (1.0 / math.sqrt(2.0))))


def _sigmoid(x):
    return 1.0 / (1.0 + jnp.exp(-x))


def _rg_kernel(x_ref, conv0_ref, h0_ref, g_ref, win_ref, cw_ref, cb_ref, wa_ref, ba_ref,
               wx_ref, bx_ref, lam_ref, wout_ref,
               y_ref, conv_ref, hlast_ref,
               xbuf_ref, hcar_ref, *, stride, steps, pad):
    rows = stride * steps
    tail = (CONV_W - 1) * stride
    c = pl.program_id(1)

    @pl.when(c == 0)
    def _():
        xbuf_ref[0:pad, :] = conv0_ref[0]
        hcar_ref[...] = h0_ref[0]

    x = x_ref[0]
    xn = _rms(x, g_ref[...]).astype(BF16)
    proj = jnp.dot(xn, win_ref[...], preferred_element_type=F32)
    gate = _gelu_tanh(proj[:, :D_RNN])
    xr = proj[:, D_RNN:]
    xbuf_ref[pad:pad + rows, :] = xr

    y = cb_ref[...] + cw_ref[0:1, :] * xbuf_ref[pad - 3 * stride:pad - 3 * stride + rows, :]
    y = y + cw_ref[1:2, :] * xbuf_ref[pad - 2 * stride:pad - 2 * stride + rows, :]
    y = y + cw_ref[2:3, :] * xbuf_ref[pad - stride:pad - stride + rows, :]
    xc = y + cw_ref[3:4, :] * xr

    conv_ref[0] = xbuf_ref[pad + rows - tail:pad + rows, :]
    xbuf_ref[0:pad, :] = xbuf_ref[rows:rows + pad, :]

    xcb = xc.astype(BF16)
    rs, is_ = [], []
    for n in range(RG_BLOCKS):
        blk = xcb[:, n * RG_BLOCK_W:(n + 1) * RG_BLOCK_W]
        rs.append(jnp.dot(blk, wa_ref[n], preferred_element_type=F32))
        is_.append(jnp.dot(blk, wx_ref[n], preferred_element_type=F32))
    r = _sigmoid(jnp.concatenate(rs, axis=1) + ba_ref[...])
    i = _sigmoid(jnp.concatenate(is_, axis=1) + bx_ref[...])

    nlam = -lam_ref[...]
    softplus = jnp.maximum(nlam, 0.0) + jnp.log1p(jnp.exp(-jnp.abs(nlam)))
    log_a = -RG_C * r * softplus
    a = jnp.exp(log_a)
    mult = jnp.sqrt(-jnp.tanh(log_a) * (a * a + 1.0))
    b = mult * i * xc

    row = lax.broadcasted_iota(jnp.int32, (rows, D_RNN), 0)
    s = 1
    while s < steps:
        sh = s * stride
        a_sh = pltpu.roll(a, sh, axis=0)
        b_sh = pltpu.roll(b, sh, axis=0)
        m = row >= sh
        b = jnp.where(m, a * b_sh + b, b)
        a = jnp.where(m, a * a_sh, a)
        s *= 2
    if stride == 1:
        h = a * hcar_ref[...] + b
    else:
        hc = hcar_ref[...]
        h = a * jnp.concatenate([hc] * steps, axis=0) + b
    hl = h[rows - stride:rows, :]
    hcar_ref[...] = hl
    hlast_ref[0] = hl

    hg = (h * gate).astype(BF16)
    y_ref[0] = x + jnp.dot(hg, wout_ref[...], preferred_element_type=F32)


def _rg_layer(x3, conv0, h0, w, *, stride, steps):
    g_, r_, d_ = x3.shape
    rows = stride * steps
    assert r_ % rows == 0
    pad = conv0.shape[1]
    tail = (CONV_W - 1) * stride
    nchunks = r_ // rows
    const2 = lambda g, c: (0, 0)
    const3 = lambda g, c: (0, 0, 0)
    kern = functools.partial(_rg_kernel, stride=stride, steps=steps, pad=pad)
    return pl.pallas_call(
        kern,
        grid=(g_, nchunks),
        in_specs=[
            pl.BlockSpec((1, rows, d_), lambda g, c: (g, c, 0)),
            pl.BlockSpec((1, pad, d_), lambda g, c: (g, 0, 0)),
            pl.BlockSpec((1, stride, d_), lambda g, c: (g, 0, 0)),
            pl.BlockSpec((1, d_), const2),
            pl.BlockSpec((d_, 2 * D_RNN), const2),
            pl.BlockSpec((CONV_W, D_RNN), const2),
            pl.BlockSpec((1, D_RNN), const2),
            pl.BlockSpec((RG_BLOCKS, RG_BLOCK_W, RG_BLOCK_W), const3),
            pl.BlockSpec((1, D_RNN), const2),
            pl.BlockSpec((RG_BLOCKS, RG_BLOCK_W, RG_BLOCK_W), const3),
            pl.BlockSpec((1, D_RNN), const2),
            pl.BlockSpec((1, D_RNN), const2),
            pl.BlockSpec((D_RNN, d_), const2),
        ],
        out_specs=[
            pl.BlockSpec((1, rows, d_), lambda g, c: (g, c, 0)),
            pl.BlockSpec((1, tail, D_RNN), lambda g, c: (g, 0, 0)),
            pl.BlockSpec((1, stride, D_RNN), lambda g, c: (g, 0, 0)),
        ],
        out_shape=[
            jax.ShapeDtypeStruct((g_, r_, d_), F32),
            jax.ShapeDtypeStruct((g_, tail, D_RNN), F32),
            jax.ShapeDtypeStruct((g_, stride, D_RNN), F32),
        ],
        scratch_shapes=[
            pltpu.VMEM((pad + rows, D_RNN), F32),
            pltpu.VMEM((stride, D_RNN), F32),
        ],
        compiler_params=_cparams("arbitrary", "arbitrary"),
        name="rg_layer",
    )(x3, conv0, h0, w["g"], w["w_in"], w["conv_w"], w["conv_b"], w["wa"], w["ba"],
      w["wx"], w["bx"], w["lam"], w["w_out"])


def _ce(v, i, j):
    hi = jnp.maximum(v[i], v[j])
    lo = jnp.minimum(v[i], v[j])
    v[i], v[j] = hi, lo


def _sort16(v):
    n = 16
    k = 2
    while k <= n:
        j = k // 2
        while j >= 1:
            for i in range(n):
                l = i ^ j
                if l > i:
                    if (i & k) == 0:
                        _ce(v, i, l)
                    else:
                        _ce(v, l, i)
            j //= 2
        k *= 2


def _merge_top16(a, b):
    v = [jnp.maximum(a[k], b[15 - k]) for k in range(16)]
    j = 8
    while j >= 1:
        for i in range(16):
            l = i ^ j
            if l > i:
                _ce(v, i, l)
        j //= 2
    return v


def _top16(vs):
    groups = []
    for g in range(len(vs) // 16):
        lst = list(vs[16 * g:16 * g + 16])
        _sort16(lst)
        groups.append(lst)
    while len(groups) > 1:
        groups = [_merge_top16(groups[i], groups[i + 1]) for i in range(0, len(groups), 2)]
    return groups[0]


def _peer_select(sc_ref, lt):
    sv = []
    for p in range(2):
        vs = [sc_ref[p, lt, pl.ds(n, PEER_HEADS, stride=N_KEYS), :] for n in range(N_KEYS)]
        sv.append(_top16(vs))
    sv0, sv1 = sv
    first = [sv0[0] + sv1[j] for j in range(PEER_TOPK)]
    rest = []
    for i in range(1, PEER_TOPK):
        for j in range(PEER_TOPK // (i + 1)):
            rest.append(sv0[i] + sv1[j])
    cands = first + rest
    neg = jnp.full_like(first[0], -jnp.inf)
    while len(rest) % 16:
        rest.append(neg)
    top = _top16(first + rest)
    tau = top[PEER_TOPK - 1]
    mx = first[0]
    z = jnp.zeros_like(mx)
    for cnd in cands:
        z = z + jnp.where(cnd >= tau, jnp.exp(cnd - mx), 0.0)
    return tau, mx + jnp.log(z)


def _peer_front_kernel(*refs, tm, with_attn):
    if with_attn:
        (x_ref, o_ref, wo_ref, g_ref, wqt_ref, sk_ref,
         x1_ref, xnt_ref, s0_ref, s1_ref, tau_ref, cc_ref, sc_ref) = refs
        x1 = x_ref[...] + jnp.dot(o_ref[...].astype(BF16), wo_ref[...], preferred_element_type=F32)
        x1_ref[...] = x1
    else:
        (x_ref, g_ref, wqt_ref, sk_ref, xnt_ref, s0_ref, s1_ref, tau_ref, cc_ref, sc_ref) = refs
        x1 = x_ref[...]
    xn = _rms(x1, g_ref[...])
    xnt = xn.T.astype(BF16)
    xnt_ref[...] = xnt
    qt = jnp.dot(wqt_ref[...], xnt, preferred_element_type=F32).astype(BF16)
    for p, s_ref in enumerate((s0_ref, s1_ref)):
        for h in range(PEER_HEADS):
            base = p * (PQ_DIM // 2) + h * (D_KEY // 2)
            s = jnp.dot(sk_ref[p], qt[base:base + D_KEY // 2, :], preferred_element_type=F32)
            s_ref[h * N_KEYS:(h + 1) * N_KEYS, :] = s
            for lt in range(tm // V7X_LANES):
                sc_ref[p, lt, h * N_KEYS:(h + 1) * N_KEYS, :] = s[:, lt * V7X_LANES:(lt + 1) * V7X_LANES]
    for lt in range(tm // V7X_LANES):
        lanes = pl.ds(lt * V7X_LANES, V7X_LANES)
        tau, cc = _peer_select(sc_ref, lt)
        tau_ref[:, lanes] = tau
        cc_ref[:, lanes] = cc


def _peer_front(x, w, attn=None, *, tm):
    n_, d_ = x.shape
    assert n_ % tm == 0
    with_attn = attn is not None
    const2 = lambda i: (0, 0)
    const3 = lambda i: (0, 0, 0)
    tok = pl.BlockSpec((tm, d_), lambda i: (i, 0))
    in_specs = [tok]
    args = [x]
    if with_attn:
        in_specs += [pl.BlockSpec((tm, Q_DIM), lambda i: (i, 0)), pl.BlockSpec((Q_DIM, d_), const2)]
        args += list(attn)
    in_specs += [
        pl.BlockSpec((1, d_), const2),
        pl.BlockSpec((PQ_DIM, d_), const2),
        pl.BlockSpec((2, N_KEYS, D_KEY // 2), const3),
    ]
    args += [w["g"], w["wqt"], w["sk"]]
    feat = lambda rows: pl.BlockSpec((rows, tm), lambda i: (0, i))
    out_specs = [feat(d_), feat(PEER_HEADS * N_KEYS), feat(PEER_HEADS * N_KEYS),
                 feat(PEER_HEADS), feat(PEER_HEADS)]
    out_shape = [
        jax.ShapeDtypeStruct((d_, n_), BF16),
        jax.ShapeDtypeStruct((PEER_HEADS * N_KEYS, n_), F32),
        jax.ShapeDtypeStruct((PEER_HEADS * N_KEYS, n_), F32),
        jax.ShapeDtypeStruct((PEER_HEADS, n_), F32),
        jax.ShapeDtypeStruct((PEER_HEADS, n_), F32),
    ]
    if with_attn:
        out_specs = [tok] + out_specs
        out_shape = [jax.ShapeDtypeStruct((n_, d_), F32)] + out_shape
    outs = pl.pallas_call(
        functools.partial(_peer_front_kernel, tm=tm, with_attn=with_attn),
        grid=(n_ // tm,),
        in_specs=in_specs,
        out_specs=out_specs,
        out_shape=out_shape,
        scratch_shapes=[pltpu.VMEM((2, tm // V7X_LANES, PEER_HEADS * N_KEYS, V7X_LANES), F32)],
        compiler_params=_cparams("arbitrary"),
        name="peer_front_attn" if with_attn else "peer_front",
    )(*args)
    if with_attn:
        return outs[0], outs[1:]
    return x, outs


def _peer_dense_kernel(xnt_ref, u_ref, vt_ref, s0_ref, s1_ref, tau_ref, cc_ref, x1_ref, gf_ref,
                       out_ref, acc_ref, ht_ref, pt_ref, *, eb, final_norm):
    j = pl.program_id(1)
    nj = pl.num_programs(1)
    rb = 2 * V7X_SUBLANES

    @pl.when(j == 0)
    def _():
        acc_ref[...] = jnp.zeros_like(acc_ref)

    ht_ref[...] = jnp.dot(u_ref[...], xnt_ref[...], preferred_element_type=F32)

    def a_body(al, carry):
        a = j * (eb // N_KEYS) + al
        s0_rows = [s0_ref[pl.ds(h * N_KEYS + a, 1), :] for h in range(PEER_HEADS)]
        for bg in range(N_KEYS // rb):
            w = None
            for h in range(PEER_HEADS):
                z = s0_rows[h] + s1_ref[h * N_KEYS + bg * rb:h * N_KEYS + (bg + 1) * rb, :]
                wh = jnp.where(z >= tau_ref[h:h + 1, :], jnp.exp(z - cc_ref[h:h + 1, :]), 0.0)
                w = wh if w is None else w + wh
            rows = pl.ds(pl.multiple_of(al * N_KEYS + bg * rb, rb), rb)
            pt_ref[rows, :] = (_gelu_erf(ht_ref[rows, :]) * w).astype(BF16)
        return carry

    lax.fori_loop(0, eb // N_KEYS, a_body, 0)
    acc_ref[...] += jnp.dot(vt_ref[...], pt_ref[...], preferred_element_type=F32)

    @pl.when(j == nj - 1)
    def _():
        y = x1_ref[...] + acc_ref[...].T
        if final_norm:
            y = _rms(y, gf_ref[...])
        out_ref[...] = y


def _peer_dense(x1, front, w, gf, *, tm, eb, final_norm):
    xnt, s0t, s1t, tau, cc = front
    n_, d_ = x1.shape
    assert n_ % tm == 0 and N_EXPERTS % eb == 0 and eb % N_KEYS == 0
    feat = lambda rows: pl.BlockSpec((rows, tm), lambda i, j: (0, i))
    return pl.pallas_call(
        functools.partial(_peer_dense_kernel, eb=eb, final_norm=final_norm),
        grid=(n_ // tm, N_EXPERTS // eb),
        in_specs=[
            feat(d_),
            pl.BlockSpec((eb, d_), lambda i, j: (j, 0)),
            pl.BlockSpec((d_, eb), lambda i, j: (0, j)),
            feat(PEER_HEADS * N_KEYS),
            feat(PEER_HEADS * N_KEYS),
            feat(PEER_HEADS),
            feat(PEER_HEADS),
            pl.BlockSpec((tm, d_), lambda i, j: (i, 0)),
            pl.BlockSpec((1, d_), lambda i, j: (0, 0)),
        ],
        out_specs=pl.BlockSpec((tm, d_), lambda i, j: (i, 0)),
        out_shape=jax.ShapeDtypeStruct((n_, d_), F32),
        scratch_shapes=[
            pltpu.VMEM((d_, tm), F32),
            pltpu.VMEM((eb, tm), F32),
            pltpu.VMEM((eb, tm), BF16),
        ],
        compiler_params=_cparams("arbitrary", "arbitrary"),
        name="peer_dense",
    )(xnt, w["u"], w["vt"], s0t, s1t, tau, cc, x1, gf)


def _rope(x, cos, sin_signed):
    half = HEAD_DIM // 2
    lane = lax.broadcasted_iota(jnp.int32, (1, V7X_LANES), 1)
    first = (lane % HEAD_DIM) < half
    outs = []
    for t in range(x.shape[1] // V7X_LANES):
        xt = x[:, t * V7X_LANES:(t + 1) * V7X_LANES]
        swapped = jnp.where(first, pltpu.roll(xt, V7X_LANES - half, axis=1), pltpu.roll(xt, half, axis=1))
        outs.append(xt * cos + swapped * sin_signed)
    return jnp.concatenate(outs, axis=1)


def _qkv_kernel(x_ref, gkv_ref, wkv_ref, gq_ref, wq_ref, cos_ref, sin_ref, q_ref, k_ref, v_ref):
    x = x_ref[...]
    cos = cos_ref[...]
    sin = sin_ref[...]
    kv = jnp.dot(_rms(x, gkv_ref[...]).astype(BF16), wkv_ref[...], preferred_element_type=F32)
    k_ref[...] = _rope(kv[:, :KV_DIM], cos, sin)
    v_ref[...] = kv[:, KV_DIM:]
    q = jnp.dot(_rms(x, gq_ref[...]).astype(BF16), wq_ref[...], preferred_element_type=F32)
    q_ref[...] = (_rope(q, cos, sin) * (HEAD_DIM ** -0.5)).astype(q_ref.dtype)


def _qkv(x, w, cos, sin, *, tm, table_blocks, q_dtype):
    n_, d_ = x.shape
    assert n_ % tm == 0
    const2 = lambda i: (0, 0)
    tok = lambda cols: pl.BlockSpec((tm, cols), lambda i: (i, 0))
    tab = pl.BlockSpec((tm, V7X_LANES), lambda i: (i % table_blocks, 0))
    return pl.pallas_call(
        _qkv_kernel,
        grid=(n_ // tm,),
        in_specs=[tok(d_), pl.BlockSpec((1, d_), const2), pl.BlockSpec((d_, 2 * KV_DIM), const2),
                  pl.BlockSpec((1, d_), const2), pl.BlockSpec((d_, Q_DIM), const2), tab, tab],
        out_specs=[tok(Q_DIM), tok(KV_DIM), tok(KV_DIM)],
        out_shape=[jax.ShapeDtypeStruct((n_, Q_DIM), q_dtype),
                   jax.ShapeDtypeStruct((n_, KV_DIM), F32),
                   jax.ShapeDtypeStruct((n_, KV_DIM), F32)],
        compiler_params=_cparams("arbitrary"),
        name="qkv_rope",
    )(x, w["g_kv"], w["w_kv"], w["g_q"], w["w_q"], cos, sin)


def _sink_attend(s, mask, sink, v):
    s = jnp.where(mask, s, -jnp.inf)
    m = jnp.maximum(jnp.max(s, axis=-1, keepdims=True), sink)
    p = jnp.exp(s - m)
    denom = jnp.sum(p, axis=-1, keepdims=True) + jnp.exp(sink - m)
    return jnp.dot(p.astype(BF16), v, preferred_element_type=F32) / denom


def _swa_prompt_kernel(sink_ref, q_ref, kp_ref, kc_ref, vp_ref, vc_ref, o_ref):
    n = pl.program_id(1)
    c = WINDOW
    k2 = jnp.concatenate([kp_ref[...], kc_ref[...]], axis=0).astype(BF16)
    v2 = jnp.concatenate([vp_ref[...], vc_ref[...]], axis=0).astype(BF16)
    qi = lax.broadcasted_iota(jnp.int32, (c, 2 * c), 0)
    kj = lax.broadcasted_iota(jnp.int32, (c, 2 * c), 1)
    mask = (kj > qi) & (kj <= qi + c) & ((kj >= c) | (n > 0))
    outs = []
    for hd in range(N_HEADS):
        kvh = hd // GROUP
        q = q_ref[:, hd * HEAD_DIM:(hd + 1) * HEAD_DIM]
        kk = k2[:, kvh * HEAD_DIM:(kvh + 1) * HEAD_DIM]
        s = lax.dot_general(q, kk, (((1,), (1,)), ((), ())), preferred_element_type=F32)
        outs.append(_sink_attend(s, mask, sink_ref[hd], v2[:, kvh * HEAD_DIM:(kvh + 1) * HEAD_DIM]))
    o_ref[...] = jnp.concatenate(outs, axis=1).astype(BF16)


def _swa_prompt(q, k, v, sinks, *, batch, seq):
    nb = seq // WINDOW
    cur = lambda cols: pl.BlockSpec((WINDOW, cols), lambda b, n: (b * nb + n, 0))
    prev = lambda cols: pl.BlockSpec((WINDOW, cols), lambda b, n: (b * nb + jnp.maximum(n - 1, 0), 0))
    return pl.pallas_call(
        _swa_prompt_kernel,
        grid=(batch, nb),
        in_specs=[pl.BlockSpec(memory_space=pltpu.SMEM),
                  cur(Q_DIM), prev(KV_DIM), cur(KV_DIM), prev(KV_DIM), cur(KV_DIM)],
        out_specs=cur(Q_DIM),
        out_shape=jax.ShapeDtypeStruct((batch * seq, Q_DIM), BF16),
        compiler_params=_cparams("arbitrary", "arbitrary"),
        name="swa_prompt",
    )(sinks, q, k, k, v, v)


def _swa_sample_kernel(sink_ref, q_ref, kc_ref, kn_ref, vc_ref, vn_ref, o_ref, k2_ref, v2_ref, *, nq):
    c = WINDOW
    for src_c, src_n, dst in ((kc_ref, kn_ref, k2_ref), (vc_ref, vn_ref, v2_ref)):
        dst[0:c, :] = src_c[0]
        dst[c:2 * c, :] = jnp.zeros((c, KV_DIM), F32)
        dst[c:c + nq, :] = src_n[0]
    k2 = k2_ref[...].astype(BF16)
    v2 = v2_ref[...].astype(BF16)
    q = q_ref[0]
    rows = GROUP * nq
    qi = lax.broadcasted_iota(jnp.int32, (rows, 2 * c), 0) % nq
    kj = lax.broadcasted_iota(jnp.int32, (rows, 2 * c), 1)
    mask = ((kj < c) & (kj > qi)) | ((kj >= c) & (kj - c <= qi))
    outs = [None] * N_HEADS
    for kvh in range(N_KV_HEADS):
        heads = range(kvh * GROUP, (kvh + 1) * GROUP)
        qs = jnp.concatenate([q[:, hd * HEAD_DIM:(hd + 1) * HEAD_DIM] for hd in heads], axis=0).astype(BF16)
        sink = jnp.concatenate([jnp.full((nq, 1), sink_ref[hd], F32) for hd in heads], axis=0)
        s = lax.dot_general(qs, k2[:, kvh * HEAD_DIM:(kvh + 1) * HEAD_DIM], (((1,), (1,)), ((), ())),
                            preferred_element_type=F32)
        o = _sink_attend(s, mask, sink, v2[:, kvh * HEAD_DIM:(kvh + 1) * HEAD_DIM])
        for g, hd in enumerate(heads):
            outs[hd] = o[g * nq:(g + 1) * nq, :]
    o_ref[0] = jnp.concatenate(outs, axis=1)


def _swa_sample(q, kc, kn, vc, vn, sinks):
    b_, nq, _ = q.shape
    assert nq <= WINDOW
    blk = lambda rows, cols: pl.BlockSpec((1, rows, cols), lambda b: (b, 0, 0))
    return pl.pallas_call(
        functools.partial(_swa_sample_kernel, nq=nq),
        grid=(b_,),
        in_specs=[pl.BlockSpec(memory_space=pltpu.SMEM),
                  blk(nq, Q_DIM), blk(WINDOW, KV_DIM), blk(nq, KV_DIM), blk(WINDOW, KV_DIM), blk(nq, KV_DIM)],
        out_specs=blk(nq, Q_DIM),
        out_shape=jax.ShapeDtypeStruct((b_, nq, Q_DIM), F32),
        scratch_shapes=[pltpu.VMEM((2 * WINDOW, KV_DIM), F32), pltpu.VMEM((2 * WINDOW, KV_DIM), F32)],
        compiler_params=_cparams("arbitrary"),
        name="swa_sample",
    )(sinks, q, kc, kn, vc, vn)


def _rope_tables(pos):
    half = HEAD_DIM // 2
    inv = ROPE_THETA ** (-jnp.arange(half, dtype=F32) / half)
    ang = pos.astype(F32)[:, None] * inv[None, :]
    cos = jnp.cos(ang)
    sin = jnp.sin(ang)
    reps = V7X_LANES // HEAD_DIM
    cos_t = jnp.tile(jnp.concatenate([cos, cos], axis=1), (1, reps))
    sin_t = jnp.tile(jnp.concatenate([-sin, sin], axis=1), (1, reps))
    return cos_t, sin_t


def _pick(n, cands):
    for c in cands:
        if n % c == 0:
            return c
    raise ValueError(f"no tile for {n}")


def _peer(x1, pw, gf, attn=None, *, final_norm):
    n_ = x1.shape[0]
    x1, front = _peer_front(x1, pw, attn, tm=_pick(n_, (256, 128)))
    return _peer_dense(x1, front, pw, gf, tm=_pick(n_, (512, 256, 128)), eb=1024, final_norm=final_norm)


def kernel(x_prompt, x_sample, state_conv, state_h, cache_k, cache_v, rg_norm, rg_w_in, rg_conv_w,
           rg_conv_b, rg_wa, rg_ba, rg_wx, rg_bx, rg_lambda, rg_w_out, kv_norm, w_kv, attn_norm, w_q,
           sinks, w_o, ffn_norm, peer_wq, peer_subkeys, peer_u, peer_v, final_norm):
    d = D_MODEL
    row = lambda a: a.reshape(1, -1).astype(F32)
    rgw = dict(g=row(rg_norm[0]), w_in=rg_w_in[0].astype(BF16), conv_w=rg_conv_w[0],
               conv_b=row(rg_conv_b[0]), wa=rg_wa[0].astype(BF16), ba=row(rg_ba[0]),
               wx=rg_wx[0].astype(BF16), bx=row(rg_bx[0]), lam=row(rg_lambda[0]),
               w_out=rg_w_out[0].astype(BF16))
    peer_w = []
    for l in range(2):
        wq = peer_wq[l].reshape(d, PEER_HEADS, 2, D_KEY // 2).transpose(2, 1, 3, 0).reshape(PQ_DIM, d)
        peer_w.append(dict(g=row(ffn_norm[l]), wqt=wq.astype(BF16), sk=peer_subkeys[l].astype(BF16),
                           u=peer_u[l].astype(BF16), vt=peer_v[l].T.astype(BF16)))
    attw = dict(g_kv=row(kv_norm), w_kv=w_kv.astype(BF16), g_q=row(attn_norm[0]), w_q=w_q[0].astype(BF16))
    wo = w_o[0].astype(BF16)
    sink = sinks[0].astype(F32)
    gf = row(final_norm)

    bp, tp, _ = x_prompt.shape
    steps_p = _pick(tp, (256, 128, 64, 32, 16, 8))
    pad_p = V7X_SUBLANES
    x1, conv_p, h_p = _rg_layer(x_prompt, jnp.zeros((bp, pad_p, d), F32), jnp.zeros((bp, 1, d), F32),
                                rgw, stride=1, steps=steps_p)
    xp = _peer(x1.reshape(bp * tp, d), peer_w[0], gf, final_norm=False)
    cos_p, sin_p = _rope_tables(jnp.arange(tp))
    tm_p = _pick(tp, (512, 256, 128))
    q, k, v = _qkv(xp, attw, cos_p, sin_p, tm=tm_p, table_blocks=tp // tm_p, q_dtype=BF16)
    o = _swa_prompt(q, k, v, sink, batch=bp, seq=tp)
    y_prompt = _peer(xp, peer_w[1], gf, attn=(o, wo), final_norm=True).reshape(bp, tp, d)
    k_p = k.reshape(bp, tp, N_KV_HEADS, HEAD_DIM)[:, -WINDOW:]
    v_p = v.reshape(bp, tp, N_KV_HEADS, HEAD_DIM)[:, -WINDOW:]
    conv_p = conv_p[None]
    h_p = h_p.reshape(1, bp, d)

    bs, ts, _ = x_sample.shape
    xs = x_sample.transpose(1, 0, 2).reshape(1, ts * bs, d)
    conv0 = state_conv[0].transpose(1, 0, 2).reshape(1, (CONV_W - 1) * bs, d)
    x1s, conv_s, h_s = _rg_layer(xs, conv0, state_h[0][None], rgw, stride=bs, steps=ts)
    xs1 = _peer(x1s.reshape(ts * bs, d), peer_w[0], gf, final_norm=False)
    xs1 = xs1.reshape(ts, bs, d).transpose(1, 0, 2).reshape(bs * ts, d)
    cos_s, sin_s = _rope_tables(PAST_LEN + jnp.arange(ts))
    cos_s = jnp.tile(cos_s, (bs, 1))
    sin_s = jnp.tile(sin_s, (bs, 1))
    tm_s = _pick(bs * ts, (512, 256, 128, 64, 32, 16, 8))
    qs, ks, vs = _qkv(xs1, attw, cos_s, sin_s, tm=tm_s, table_blocks=(bs * ts) // tm_s, q_dtype=F32)
    ck = cache_k.reshape(bs, WINDOW, KV_DIM)
    cv = cache_v.reshape(bs, WINDOW, KV_DIM)
    kn = ks.reshape(bs, ts, KV_DIM)
    vn = vs.reshape(bs, ts, KV_DIM)
    os_ = _swa_sample(qs.reshape(bs, ts, Q_DIM), ck, kn, cv, vn, sink)
    y_sample = _peer(xs1, peer_w[1], gf, attn=(os_.reshape(bs * ts, Q_DIM), wo),
                     final_norm=True).reshape(bs, ts, d)
    k_s = jnp.concatenate([ck, kn], axis=1)[:, -WINDOW:].reshape(bs, WINDOW, N_KV_HEADS, HEAD_DIM)
    v_s = jnp.concatenate([cv, vn], axis=1)[:, -WINDOW:].reshape(bs, WINDOW, N_KV_HEADS, HEAD_DIM)
    conv_s = conv_s.reshape(CONV_W - 1, bs, d).transpose(1, 0, 2)[None]
    h_s = h_s.reshape(1, bs, d)

    return (y_prompt, y_sample, conv_p, h_p, k_p, v_p, conv_s, h_s, k_s, v_s)
```

```python
import functools
import math

import jax
import jax.numpy as jnp
from jax import lax
from jax.experimental import pallas as pl
from jax.experimental.pallas import tpu as pltpu

D_MODEL = 1024
PAST_LEN = 16384
D_RNN = D_MODEL
RG_BLOCKS = 8
RG_BLOCK_W = D_RNN // RG_BLOCKS
CONV_W = 4
RG_C = 8.0
HEAD_DIM = 64
N_HEADS = D_MODEL // HEAD_DIM
N_KV_HEADS = 4
GROUP = N_HEADS // N_KV_HEADS
WINDOW = 128
ROPE_THETA = 10000.0
N_KEYS = 128
N_EXPERTS = N_KEYS * N_KEYS
PEER_HEADS = 8
PEER_TOPK = 16
D_KEY = 256
EPS = 1e-6

KV_DIM = N_KV_HEADS * HEAD_DIM
Q_DIM = N_HEADS * HEAD_DIM
PQ_DIM = PEER_HEADS * D_KEY

V7X_LANES = 128
V7X_SUBLANES = 8
V7X_VMEM_LIMIT_BYTES = 56 * 1024 * 1024

BF16 = jnp.bfloat16
F32 = jnp.float32


def _cparams(*sem):
    return pltpu.CompilerParams(dimension_semantics=sem, vmem_limit_bytes=V7X_VMEM_LIMIT_BYTES)


def _rms(x, g):
    return x * lax.rsqrt(jnp.mean(x * x, axis=-1, keepdims=True) + EPS) * g


def _gelu_tanh(x):
    return 0.5 * x * (1.0 + jnp.tanh(math.sqrt(2.0 / math.pi) * (x + 0.044715 * (x * x * x))))


def _gelu_erf(x):
    return 0.5 * x * (1.0 + lax.erf(x * (1.0 / math.sqrt(2.0))))


def _sigmoid(x):
    return 1.0 / (1.0 + jnp.exp(-x))


def _rg_kernel(x_ref, conv0_ref, h0_ref, g_ref, win_ref, cw_ref, cb_ref, wa_ref, ba_ref,
               wx_ref, bx_ref, lam_ref, wout_ref,
               y_ref, conv_ref, hlast_ref,
               xbuf_ref, hcar_ref, *, stride, steps, pad):
    rows = stride * steps
    tail = (CONV_W - 1) * stride
    c = pl.program_id(1)

    @pl.when(c == 0)
    def _():
        xbuf_ref[0:pad, :] = conv0_ref[0]
        hcar_ref[...] = h0_ref[0]

    x = x_ref[0]
    xn = _rms(x, g_ref[...]).astype(BF16)
    proj = jnp.dot(xn, win_ref[...], preferred_element_type=F32)
    gate = _gelu_tanh(proj[:, :D_RNN])
    xr = proj[:, D_RNN:]
    xbuf_ref[pad:pad + rows, :] = xr

    y = cb_ref[...] + cw_ref[0:1, :] * xbuf_ref[pad - 3 * stride:pad - 3 * stride + rows, :]
    y = y + cw_ref[1:2, :] * xbuf_ref[pad - 2 * stride:pad - 2 * stride + rows, :]
    y = y + cw_ref[2:3, :] * xbuf_ref[pad - stride:pad - stride + rows, :]
    xc = y + cw_ref[3:4, :] * xr

    conv_ref[0] = xbuf_ref[pad + rows - tail:pad + rows, :]
    xbuf_ref[0:pad, :] = xbuf_ref[rows:rows + pad, :]

    xcb = xc.astype(BF16)
    rs, is_ = [], []
    for n in range(RG_BLOCKS):
        blk = xcb[:, n * RG_BLOCK_W:(n + 1) * RG_BLOCK_W]
        rs.append(jnp.dot(blk, wa_ref[n], preferred_element_type=F32))
        is_.append(jnp.dot(blk, wx_ref[n], preferred_element_type=F32))
    r = _sigmoid(jnp.concatenate(rs, axis=1) + ba_ref[...])
    i = _sigmoid(jnp.concatenate(is_, axis=1) + bx_ref[...])

    nlam = -lam_ref[...]
    softplus = jnp.maximum(nlam, 0.0) + jnp.log1p(jnp.exp(-jnp.abs(nlam)))
    log_a = -RG_C * r * softplus
    a = jnp.exp(log_a)
    mult = jnp.sqrt(-jnp.tanh(log_a) * (a * a + 1.0))
    b = mult * i * xc

    row = lax.broadcasted_iota(jnp.int32, (rows, D_RNN), 0)
    s = 1
    while s < steps:
        sh = s * stride
        a_sh = pltpu.roll(a, sh, axis=0)
        b_sh = pltpu.roll(b, sh, axis=0)
        m = row >= sh
        b = jnp.where(m, a * b_sh + b, b)
        a = jnp.where(m, a * a_sh, a)
        s *= 2
    if stride == 1:
        h = a * hcar_ref[...] + b
    else:
        hc = hcar_ref[...]
        h = a * jnp.concatenate([hc] * steps, axis=0) + b
    hl = h[rows - stride:rows, :]
    hcar_ref[...] = hl
    hlast_ref[0] = hl

    hg = (h * gate).astype(BF16)
    y_ref[0] = x + jnp.dot(hg, wout_ref[...], preferred_element_type=F32)


def _rg_layer(x3, conv0, h0, w, *, stride, steps):
    g_, r_, d_ = x3.shape
    rows = stride * steps
    assert r_ % rows == 0
    pad = conv0.shape[1]
    tail = (CONV_W - 1) * stride
    nchunks = r_ // rows
    const2 = lambda g, c: (0, 0)
    const3 = lambda g, c: (0, 0, 0)
    kern = functools.partial(_rg_kernel, stride=stride, steps=steps, pad=pad)
    return pl.pallas_call(
        kern,
        grid=(g_, nchunks),
        in_specs=[
            pl.BlockSpec((1, rows, d_), lambda g, c: (g, c, 0)),
            pl.BlockSpec((1, pad, d_), lambda g, c: (g, 0, 0)),
            pl.BlockSpec((1, stride, d_), lambda g, c: (g, 0, 0)),
            pl.BlockSpec((1, d_), const2),
            pl.BlockSpec((d_, 2 * D_RNN), const2),
            pl.BlockSpec((CONV_W, D_RNN), const2),
            pl.BlockSpec((1, D_RNN), const2),
            pl.BlockSpec((RG_BLOCKS, RG_BLOCK_W, RG_BLOCK_W), const3),
            pl.BlockSpec((1, D_RNN), const2),
            pl.BlockSpec((RG_BLOCKS, RG_BLOCK_W, RG_BLOCK_W), const3),
            pl.BlockSpec((1, D_RNN), const2),
            pl.BlockSpec((1, D_RNN), const2),
            pl.BlockSpec((D_RNN, d_), const2),
        ],
        out_specs=[
            pl.BlockSpec((1, rows, d_), lambda g, c: (g, c, 0)),
            pl.BlockSpec((1, tail, D_RNN), lambda g, c: (g, 0, 0)),
            pl.BlockSpec((1, stride, D_RNN), lambda g, c: (g, 0, 0)),
        ],
        out_shape=[
            jax.ShapeDtypeStruct((g_, r_, d_), F32),
            jax.ShapeDtypeStruct((g_, tail, D_RNN), F32),
            jax.ShapeDtypeStruct((g_, stride, D_RNN), F32),
        ],
        scratch_shapes=[
            pltpu.VMEM((pad + rows, D_RNN), F32),
            pltpu.VMEM((stride, D_RNN), F32),
        ],
        compiler_params=_cparams("arbitrary", "arbitrary"),
        name="rg_layer",
    )(x3, conv0, h0, w["g"], w["w_in"], w["conv_w"], w["conv_b"], w["wa"], w["ba"],
      w["wx"], w["bx"], w["lam"], w["w_out"])


def _ce(v, i, j):
    hi = jnp.maximum(v[i], v[j])
    lo = jnp.minimum(v[i], v[j])
    v[i], v[j] = hi, lo


def _sort16(v):
    n = 16
    k = 2
    while k <= n:
        j = k // 2
        while j >= 1:
            for i in range(n):
                l = i ^ j
                if l > i:
                    if (i & k) == 0:
                        _ce(v, i, l)
                    else:
                        _ce(v, l, i)
            j //= 2
        k *= 2


def _merge_top16(a, b):
    v = [jnp.maximum(a[k], b[15 - k]) for k in range(16)]
    j = 8
    while j >= 1:
        for i in range(16):
            l = i ^ j
            if l > i:
                _ce(v, i, l)
        j //= 2
    return v


def _top16(vs):
    groups = []
    for g in range(len(vs) // 16):
        lst = list(vs[16 * g:16 * g + 16])
        _sort16(lst)
        groups.append(lst)
    while len(groups) > 1:
        groups = [_merge_top16(groups[i], groups[i + 1]) for i in range(0, len(groups), 2)]
    return groups[0]


def _kth_product(sv0, sv1):
    cands = [sv0[0] * sv1[j] for j in range(PEER_TOPK)]
    for i in range(1, PEER_TOPK):
        for j in range(PEER_TOPK // (i + 1)):
            cands.append(sv0[i] * sv1[j])
    padded = list(cands)
    while len(padded) % 16:
        padded.append(jnp.full_like(cands[0], -1.0))
    return _top16(padded)[PEER_TOPK - 1], cands


def _peer_select(sc_ref, lt):
    sv = []
    for p in range(2):
        vs = [sc_ref[p, lt, pl.ds(n, PEER_HEADS, stride=N_KEYS), :] for n in range(N_KEYS)]
        sv.append(_top16(vs))
    sv0, sv1 = sv
    theta, cands = _kth_product(sv0, sv1)
    z = jnp.zeros_like(theta)
    for cnd in cands:
        z = z + jnp.where(cnd >= theta, cnd, 0.0)
    scale = 0.5 / z
    theta_n, _ = _kth_product([v * scale for v in sv0], sv1)
    return scale, theta_n


def _peer_front_kernel(*refs, tm, with_attn):
    if with_attn:
        (x_ref, o_ref, wo_ref, g_ref, wqt_ref, sk_ref,
         x1_ref, xnt_ref, t0_ref, t1_ref, th_ref, sc_ref) = refs
        x1 = x_ref[...] + jnp.dot(o_ref[...].astype(BF16), wo_ref[...], preferred_element_type=F32)
        x1_ref[...] = x1
    else:
        (x_ref, g_ref, wqt_ref, sk_ref, xnt_ref, t0_ref, t1_ref, th_ref, sc_ref) = refs
        x1 = x_ref[...]
    xn = _rms(x1, g_ref[...])
    xnt = xn.T.astype(BF16)
    xnt_ref[...] = xnt
    qt = jnp.dot(wqt_ref[...], xnt, preferred_element_type=F32).astype(BF16)
    for p in range(2):
        for h in range(PEER_HEADS):
            base = p * (PQ_DIM // 2) + h * (D_KEY // 2)
            s = jnp.dot(sk_ref[p], qt[base:base + D_KEY // 2, :], preferred_element_type=F32)
            t = jnp.exp(s - jnp.max(s, axis=0, keepdims=True))
            if p == 1:
                t1_ref[h * N_KEYS:(h + 1) * N_KEYS, :] = t
            for lt in range(tm // V7X_LANES):
                sc_ref[p, lt, h * N_KEYS:(h + 1) * N_KEYS, :] = t[:, lt * V7X_LANES:(lt + 1) * V7X_LANES]
    for lt in range(tm // V7X_LANES):
        lanes = pl.ds(lt * V7X_LANES, V7X_LANES)
        scale, theta = _peer_select(sc_ref, lt)
        th_ref[:, lanes] = theta
        for n in range(N_KEYS):
            t0_ref[n * PEER_HEADS:(n + 1) * PEER_HEADS, lanes] = (
                sc_ref[0, lt, pl.ds(n, PEER_HEADS, stride=N_KEYS), :] * scale)


def _peer_front(x, w, attn=None, *, tm):
    n_, d_ = x.shape
    assert n_ % tm == 0
    with_attn = attn is not None
    const2 = lambda i: (0, 0)
    const3 = lambda i: (0, 0, 0)
    tok = pl.BlockSpec((tm, d_), lambda i: (i, 0))
    in_specs = [tok]
    args = [x]
    if with_attn:
        in_specs += [pl.BlockSpec((tm, Q_DIM), lambda i: (i, 0)), pl.BlockSpec((Q_DIM, d_), const2)]
        args += list(attn)
    in_specs += [
        pl.BlockSpec((1, d_), const2),
        pl.BlockSpec((PQ_DIM, d_), const2),
        pl.BlockSpec((2, N_KEYS, D_KEY // 2), const3),
    ]
    args += [w["g"], w["wqt"], w["sk"]]
    feat = lambda rows: pl.BlockSpec((rows, tm), lambda i: (0, i))
    out_specs = [feat(d_), feat(PEER_HEADS * N_KEYS), feat(PEER_HEADS * N_KEYS), feat(PEER_HEADS)]
    out_shape = [
        jax.ShapeDtypeStruct((d_, n_), BF16),
        jax.ShapeDtypeStruct((PEER_HEADS * N_KEYS, n_), F32),
        jax.ShapeDtypeStruct((PEER_HEADS * N_KEYS, n_), F32),
        jax.ShapeDtypeStruct((PEER_HEADS, n_), F32),
    ]
    if with_attn:
        out_specs = [tok] + out_specs
        out_shape = [jax.ShapeDtypeStruct((n_, d_), F32)] + out_shape
    outs = pl.pallas_call(
        functools.partial(_peer_front_kernel, tm=tm, with_attn=with_attn),
        grid=(n_ // tm,),
        in_specs=in_specs,
        out_specs=out_specs,
        out_shape=out_shape,
        scratch_shapes=[pltpu.VMEM((2, tm // V7X_LANES, PEER_HEADS * N_KEYS, V7X_LANES), F32)],
        compiler_params=_cparams("arbitrary"),
        name="peer_front_attn" if with_attn else "peer_front",
    )(*args)
    if with_attn:
        return outs[0], outs[1:]
    return x, outs


def _peer_weight_tile(lt, ht_ref, pt_ref, t0_ref, t1_ref, th_ref, *, eb):
    sub = V7X_SUBLANES
    lanes = slice(lt * V7X_LANES, (lt + 1) * V7X_LANES)
    th = [jnp.broadcast_to(th_ref[h:h + 1, lanes], (sub, V7X_LANES)) for h in range(PEER_HEADS)]
    for al in range(eb // N_KEYS):
        ta = [jnp.broadcast_to(t0_ref[al * PEER_HEADS + h:al * PEER_HEADS + h + 1, lanes],
                               (sub, V7X_LANES)) for h in range(PEER_HEADS)]
        for bg in range(N_KEYS // (2 * sub)):
            parts = []
            for r0 in (al * N_KEYS + 2 * bg * sub, al * N_KEYS + (2 * bg + 1) * sub):
                b0 = r0 - al * N_KEYS
                w = None
                for h in range(PEER_HEADS):
                    prod = ta[h] * t1_ref[h * N_KEYS + b0:h * N_KEYS + b0 + sub, lanes]
                    wh = jnp.where(prod >= th[h], prod, 0.0)
                    w = wh if w is None else w + wh
                hv = ht_ref[r0:r0 + sub, lanes]
                parts.append(hv * (1.0 + lax.erf(hv * (1.0 / math.sqrt(2.0)))) * w)
            r0 = al * N_KEYS + 2 * bg * sub
            pt_ref[r0:r0 + 2 * sub, lanes] = jnp.concatenate(parts, axis=0).astype(BF16)


def _peer_half_step(xnt_ref, u_ref, u_rows, vt_ref, vt_cols, acc_ref, ht_w, pt_r, ht_r, pt_w,
                    t0_ref, t1_ref, th_ref, *, eb, tm):
    mxu_w = 2 * V7X_LANES
    pieces = []
    for c in range(tm // mxu_w):
        cols = slice(c * mxu_w, (c + 1) * mxu_w)

        def scores(cols=cols):
            ht_w[:, cols] = jnp.dot(u_ref[u_rows, :], xnt_ref[:, cols], preferred_element_type=F32)

        def values(cols=cols):
            acc_ref[:, cols] += jnp.dot(vt_ref[:, vt_cols], pt_r[:, cols], preferred_element_type=F32)

        pieces += [scores, values]
    for lt in range(tm // V7X_LANES):
        pieces[lt]()
        _peer_weight_tile(lt, ht_r, pt_w, t0_ref, t1_ref, th_ref, eb=eb)


def _peer_dense_kernel(xnt_ref, u_ref, vt_ref, t0a_ref, t1a_ref, tha_ref, t0b_ref, t1b_ref, thb_ref,
                       x1_ref, gf_ref, out_ref, acc_ref, ht0_ref, ht1_ref, pt0_ref, pt1_ref,
                       *, eb, ne, final_norm):
    g = pl.program_id(0)
    tm = acc_ref.shape[1]
    fv = 2 * g - 2

    @pl.when(g == 0)
    def _():
        ht1_ref[...] = jnp.zeros_like(ht1_ref)
        pt0_ref[...] = jnp.zeros_like(pt0_ref)

    @pl.when((g == 0) | (lax.rem(fv, ne) == 0))
    def _():
        acc_ref[...] = jnp.zeros_like(acc_ref)

    _peer_half_step(xnt_ref, u_ref, slice(0, eb), vt_ref, slice(0, eb), acc_ref,
                    ht0_ref, pt0_ref, ht1_ref, pt1_ref, t0a_ref, t1a_ref, tha_ref, eb=eb, tm=tm)
    _peer_half_step(xnt_ref, u_ref, slice(eb, 2 * eb), vt_ref, slice(eb, 2 * eb), acc_ref,
                    ht1_ref, pt1_ref, ht0_ref, pt0_ref, t0b_ref, t1b_ref, thb_ref, eb=eb, tm=tm)

    @pl.when((g > 0) & (lax.rem(fv + 1, ne) == ne - 1))
    def _():
        y = x1_ref[...] + acc_ref[...].T
        if final_norm:
            y = _rms(y, gf_ref[...])
        out_ref[...] = y


def _peer_dense(x1, front, w, gf, *, tm, eb, final_norm):
    xnt, t0, t1, th = front
    n_, d_ = x1.shape
    ne = N_EXPERTS // eb
    assert n_ % tm == 0 and N_EXPERTS % eb == 0 and eb % N_KEYS == 0 and ne % 2 == 0
    assert tm % (2 * V7X_LANES) == 0
    nblocks = (n_ // tm) * ne
    last = nblocks - 1
    tok_mm1 = lambda g: jnp.minimum(2 * g, last) // ne
    tok_a = lambda g: jnp.clip(2 * g - 1, 0, last) // ne
    tok_v = lambda g: jnp.maximum(2 * g - 2, 0) // ne
    blk_a = lambda g: jnp.clip(2 * g - 1, 0, last) % ne
    blk_b = lambda g: jnp.minimum(2 * g, last) % ne
    pair_mm1 = lambda g: (jnp.minimum(2 * g, last) % ne) // 2
    pair_v = lambda g: (jnp.maximum(2 * g - 2, 0) % ne) // 2
    feat = lambda rows, tok: pl.BlockSpec((rows, tm), lambda g: (0, tok(g)))
    nk = PEER_HEADS * N_KEYS
    t0_rows = (eb // N_KEYS) * PEER_HEADS
    return pl.pallas_call(
        functools.partial(_peer_dense_kernel, eb=eb, ne=ne, final_norm=final_norm),
        grid=(nblocks // 2 + 1,),
        in_specs=[
            feat(d_, tok_mm1),
            pl.BlockSpec((2 * eb, d_), lambda g: (pair_mm1(g), 0)),
            pl.BlockSpec((d_, 2 * eb), lambda g: (0, pair_v(g))),
            pl.BlockSpec((t0_rows, tm), lambda g: (blk_a(g), tok_a(g))), feat(nk, tok_a), feat(PEER_HEADS, tok_a),
            pl.BlockSpec((t0_rows, tm), lambda g: (blk_b(g), tok_mm1(g))), feat(nk, tok_mm1),
            feat(PEER_HEADS, tok_mm1),
            pl.BlockSpec((tm, d_), lambda g: (tok_v(g), 0)),
            pl.BlockSpec((1, d_), lambda g: (0, 0)),
        ],
        out_specs=pl.BlockSpec((tm, d_), lambda g: (tok_v(g), 0)),
        out_shape=jax.ShapeDtypeStruct((n_, d_), F32),
        scratch_shapes=[
            pltpu.VMEM((d_, tm), F32),
            pltpu.VMEM((eb, tm), F32), pltpu.VMEM((eb, tm), F32),
            pltpu.VMEM((eb, tm), BF16), pltpu.VMEM((eb, tm), BF16),
        ],
        compiler_params=_cparams("arbitrary"),
        name="peer_dense",
    )(xnt, w["u"], w["vt"], t0, t1, th, t0, t1, th, x1, gf)


def _rope(x, cos, sin_signed):
    half = HEAD_DIM // 2
    lane = lax.broadcasted_iota(jnp.int32, (1, V7X_LANES), 1)
    first = (lane % HEAD_DIM) < half
    outs = []
    for t in range(x.shape[1] // V7X_LANES):
        xt = x[:, t * V7X_LANES:(t + 1) * V7X_LANES]
        swapped = jnp.where(first, pltpu.roll(xt, V7X_LANES - half, axis=1), pltpu.roll(xt, half, axis=1))
        outs.append(xt * cos + swapped * sin_signed)
    return jnp.concatenate(outs, axis=1)


def _qkv_kernel(x_ref, gkv_ref, wkv_ref, gq_ref, wq_ref, cos_ref, sin_ref, q_ref, k_ref, v_ref):
    x = x_ref[...]
    cos = cos_ref[...]
    sin = sin_ref[...]
    kv = jnp.dot(_rms(x, gkv_ref[...]).astype(BF16), wkv_ref[...], preferred_element_type=F32)
    k_ref[...] = _rope(kv[:, :KV_DIM], cos, sin)
    v_ref[...] = kv[:, KV_DIM:]
    q = jnp.dot(_rms(x, gq_ref[...]).astype(BF16), wq_ref[...], preferred_element_type=F32)
    q_ref[...] = (_rope(q, cos, sin) * (HEAD_DIM ** -0.5)).astype(q_ref.dtype)


def _qkv(x, w, cos, sin, *, tm, table_blocks, q_dtype):
    n_, d_ = x.shape
    assert n_ % tm == 0
    const2 = lambda i: (0, 0)
    tok = lambda cols: pl.BlockSpec((tm, cols), lambda i: (i, 0))
    tab = pl.BlockSpec((tm, V7X_LANES), lambda i: (i % table_blocks, 0))
    return pl.pallas_call(
        _qkv_kernel,
        grid=(n_ // tm,),
        in_specs=[tok(d_), pl.BlockSpec((1, d_), const2), pl.BlockSpec((d_, 2 * KV_DIM), const2),
                  pl.BlockSpec((1, d_), const2), pl.BlockSpec((d_, Q_DIM), const2), tab, tab],
        out_specs=[tok(Q_DIM), tok(KV_DIM), tok(KV_DIM)],
        out_shape=[jax.ShapeDtypeStruct((n_, Q_DIM), q_dtype),
                   jax.ShapeDtypeStruct((n_, KV_DIM), F32),
                   jax.ShapeDtypeStruct((n_, KV_DIM), F32)],
        compiler_params=_cparams("arbitrary"),
        name="qkv_rope",
    )(x, w["g_kv"], w["w_kv"], w["g_q"], w["w_q"], cos, sin)


def _sink_attend(s, mask, sink, v):
    s = jnp.where(mask, s, -jnp.inf)
    m = jnp.maximum(jnp.max(s, axis=-1, keepdims=True), sink)
    p = jnp.exp(s - m)
    denom = jnp.sum(p, axis=-1, keepdims=True) + jnp.exp(sink - m)
    return jnp.dot(p.astype(BF16), v, preferred_element_type=F32) / denom


def _swa_prompt_kernel(sink_ref, q_ref, kp_ref, kc_ref, vp_ref, vc_ref, o_ref):
    n = pl.program_id(1)
    c = WINDOW
    k2 = jnp.concatenate([kp_ref[...], kc_ref[...]], axis=0).astype(BF16)
    v2 = jnp.concatenate([vp_ref[...], vc_ref[...]], axis=0).astype(BF16)
    qi = lax.broadcasted_iota(jnp.int32, (c, 2 * c), 0)
    kj = lax.broadcasted_iota(jnp.int32, (c, 2 * c), 1)
    mask = (kj > qi) & (kj <= qi + c) & ((kj >= c) | (n > 0))
    outs = []
    for hd in range(N_HEADS):
        kvh = hd // GROUP
        q = q_ref[:, hd * HEAD_DIM:(hd + 1) * HEAD_DIM]
        kk = k2[:, kvh * HEAD_DIM:(kvh + 1) * HEAD_DIM]
        s = lax.dot_general(q, kk, (((1,), (1,)), ((), ())), preferred_element_type=F32)
        outs.append(_sink_attend(s, mask, sink_ref[hd], v2[:, kvh * HEAD_DIM:(kvh + 1) * HEAD_DIM]))
    o_ref[...] = jnp.concatenate(outs, axis=1).astype(BF16)


def _swa_prompt(q, k, v, sinks, *, batch, seq):
    nb = seq // WINDOW
    cur = lambda cols: pl.BlockSpec((WINDOW, cols), lambda b, n: (b * nb + n, 0))
    prev = lambda cols: pl.BlockSpec((WINDOW, cols), lambda b, n: (b * nb + jnp.maximum(n - 1, 0), 0))
    return pl.pallas_call(
        _swa_prompt_kernel,
        grid=(batch, nb),
        in_specs=[pl.BlockSpec(memory_space=pltpu.SMEM),
                  cur(Q_DIM), prev(KV_DIM), cur(KV_DIM), prev(KV_DIM), cur(KV_DIM)],
        out_specs=cur(Q_DIM),
        out_shape=jax.ShapeDtypeStruct((batch * seq, Q_DIM), BF16),
        compiler_params=_cparams("arbitrary", "arbitrary"),
        name="swa_prompt",
    )(sinks, q, k, k, v, v)


def _swa_sample_kernel(sink_ref, q_ref, kc_ref, kn_ref, vc_ref, vn_ref, o_ref, k2_ref, v2_ref, *, nq):
    c = WINDOW
    for src_c, src_n, dst in ((kc_ref, kn_ref, k2_ref), (vc_ref, vn_ref, v2_ref)):
        dst[0:c, :] = src_c[0]
        dst[c:2 * c, :] = jnp.zeros((c, KV_DIM), F32)
        dst[c:c + nq, :] = src_n[0]
    k2 = k2_ref[...].astype(BF16)
    v2 = v2_ref[...].astype(BF16)
    q = q_ref[0]
    rows = GROUP * nq
    qi = lax.broadcasted_iota(jnp.int32, (rows, 2 * c), 0) % nq
    kj = lax.broadcasted_iota(jnp.int32, (rows, 2 * c), 1)
    mask = ((kj < c) & (kj > qi)) | ((kj >= c) & (kj - c <= qi))
    outs = [None] * N_HEADS
    for kvh in range(N_KV_HEADS):
        heads = range(kvh * GROUP, (kvh + 1) * GROUP)
        qs = jnp.concatenate([q[:, hd * HEAD_DIM:(hd + 1) * HEAD_DIM] for hd in heads], axis=0).astype(BF16)
        sink = jnp.concatenate([jnp.full((nq, 1), sink_ref[hd], F32) for hd in heads], axis=0)
        s = lax.dot_general(qs, k2[:, kvh * HEAD_DIM:(kvh + 1) * HEAD_DIM], (((1,), (1,)), ((), ())),
                            preferred_element_type=F32)
        o = _sink_attend(s, mask, sink, v2[:, kvh * HEAD_DIM:(kvh + 1) * HEAD_DIM])
        for g, hd in enumerate(heads):
            outs[hd] = o[g * nq:(g + 1) * nq, :]
    o_ref[0] = jnp.concatenate(outs, axis=1)


def _swa_sample(q, kc, kn, vc, vn, sinks):
    b_, nq, _ = q.shape
    assert nq <= WINDOW
    blk = lambda rows, cols: pl.BlockSpec((1, rows, cols), lambda b: (b, 0, 0))
    return pl.pallas_call(
        functools.partial(_swa_sample_kernel, nq=nq),
        grid=(b_,),
        in_specs=[pl.BlockSpec(memory_space=pltpu.SMEM),
                  blk(nq, Q_DIM), blk(WINDOW, KV_DIM), blk(nq, KV_DIM), blk(WINDOW, KV_DIM), blk(nq, KV_DIM)],
        out_specs=blk(nq, Q_DIM),
        out_shape=jax.ShapeDtypeStruct((b_, nq, Q_DIM), F32),
        scratch_shapes=[pltpu.VMEM((2 * WINDOW, KV_DIM), F32), pltpu.VMEM((2 * WINDOW, KV_DIM), F32)],
        compiler_params=_cparams("arbitrary"),
        name="swa_sample",
    )(sinks, q, kc, kn, vc, vn)


def _rope_tables(pos):
    half = HEAD_DIM // 2
    inv = ROPE_THETA ** (-jnp.arange(half, dtype=F32) / half)
    ang = pos.astype(F32)[:, None] * inv[None, :]
    cos = jnp.cos(ang)
    sin = jnp.sin(ang)
    reps = V7X_LANES // HEAD_DIM
    cos_t = jnp.tile(jnp.concatenate([cos, cos], axis=1), (1, reps))
    sin_t = jnp.tile(jnp.concatenate([-sin, sin], axis=1), (1, reps))
    return cos_t, sin_t


def _pick(n, cands):
    for c in cands:
        if n % c == 0:
            return c
    raise ValueError(f"no tile for {n}")


def _peer(x1, pw, gf, attn=None, *, final_norm):
    n_ = x1.shape[0]
    x1, front = _peer_front(x1, pw, attn, tm=_pick(n_, (256, 128)))
    return _peer_dense(x1, front, pw, gf, tm=_pick(n_, (512, 256, 128)), eb=1024, final_norm=final_norm)


def kernel(x_prompt, x_sample, state_conv, state_h, cache_k, cache_v, rg_norm, rg_w_in, rg_conv_w,
           rg_conv_b, rg_wa, rg_ba, rg_wx, rg_bx, rg_lambda, rg_w_out, kv_norm, w_kv, attn_norm, w_q,
           sinks, w_o, ffn_norm, peer_wq, peer_subkeys, peer_u, peer_v, final_norm):
    d = D_MODEL
    row = lambda a: a.reshape(1, -1).astype(F32)
    rgw = dict(g=row(rg_norm[0]), w_in=rg_w_in[0].astype(BF16), conv_w=rg_conv_w[0],
               conv_b=row(rg_conv_b[0]), wa=rg_wa[0].astype(BF16), ba=row(rg_ba[0]),
               wx=rg_wx[0].astype(BF16), bx=row(rg_bx[0]), lam=row(rg_lambda[0]),
               w_out=rg_w_out[0].astype(BF16))
    peer_w = []
    for l in range(2):
        wq = peer_wq[l].reshape(d, PEER_HEADS, 2, D_KEY // 2).transpose(2, 1, 3, 0).reshape(PQ_DIM, d)
        peer_w.append(dict(g=row(ffn_norm[l]), wqt=wq.astype(BF16), sk=peer_subkeys[l].astype(BF16),
                           u=peer_u[l].astype(BF16), vt=peer_v[l].T.astype(BF16)))
    attw = dict(g_kv=row(kv_norm), w_kv=w_kv.astype(BF16), g_q=row(attn_norm[0]), w_q=w_q[0].astype(BF16))
    wo = w_o[0].astype(BF16)
    sink = sinks[0].astype(F32)
    gf = row(final_norm)

    bp, tp, _ = x_prompt.shape
    steps_p = _pick(tp, (256, 128, 64, 32, 16, 8))
    pad_p = V7X_SUBLANES
    x1, conv_p, h_p = _rg_layer(x_prompt, jnp.zeros((bp, pad_p, d), F32), jnp.zeros((bp, 1, d), F32),
                                rgw, stride=1, steps=steps_p)
    xp = _peer(x1.reshape(bp * tp, d), peer_w[0], gf, final_norm=False)
    cos_p, sin_p = _rope_tables(jnp.arange(tp))
    tm_p = _pick(tp, (512, 256, 128))
    q, k, v = _qkv(xp, attw, cos_p, sin_p, tm=tm_p, table_blocks=tp // tm_p, q_dtype=BF16)
    o = _swa_prompt(q, k, v, sink, batch=bp, seq=tp)
    y_prompt = _peer(xp, peer_w[1], gf, attn=(o, wo), final_norm=True).reshape(bp, tp, d)
    k_p = k.reshape(bp, tp, N_KV_HEADS, HEAD_DIM)[:, -WINDOW:]
    v_p = v.reshape(bp, tp, N_KV_HEADS, HEAD_DIM)[:, -WINDOW:]
    conv_p = conv_p[None]
    h_p = h_p.reshape(1, bp, d)

    bs, ts, _ = x_sample.shape
    xs = x_sample.transpose(1, 0, 2).reshape(1, ts * bs, d)
    conv0 = state_conv[0].transpose(1, 0, 2).reshape(1, (CONV_W - 1) * bs, d)
    x1s, conv_s, h_s = _rg_layer(xs, conv0, state_h[0][None], rgw, stride=bs, steps=ts)
    xs1 = _peer(x1s.reshape(ts * bs, d), peer_w[0], gf, final_norm=False)
    xs1 = xs1.reshape(ts, bs, d).transpose(1, 0, 2).reshape(bs * ts, d)
    cos_s, sin_s = _rope_tables(PAST_LEN + jnp.arange(ts))
    cos_s = jnp.tile(cos_s, (bs, 1))
    sin_s = jnp.tile(sin_s, (bs, 1))
    tm_s = _pick(bs * ts, (512, 256, 128, 64, 32, 16, 8))
    qs, ks, vs = _qkv(xs1, attw, cos_s, sin_s, tm=tm_s, table_blocks=(bs * ts) // tm_s, q_dtype=F32)
    ck = cache_k.reshape(bs, WINDOW, KV_DIM)
    cv = cache_v.reshape(bs, WINDOW, KV_DIM)
    kn = ks.reshape(bs, ts, KV_DIM)
    vn = vs.reshape(bs, ts, KV_DIM)
    os_ = _swa_sample(qs.reshape(bs, ts, Q_DIM), ck, kn, cv, vn, sink)
    y_sample = _peer(xs1, peer_w[1], gf, attn=(os_.reshape(bs * ts, Q_DIM), wo),
                     final_norm=True).reshape(bs, ts, d)
    k_s = jnp.concatenate([ck, kn], axis=1)[:, -WINDOW:].reshape(bs, WINDOW, N_KV_HEADS, HEAD_DIM)
    v_s = jnp.concatenate([cv, vn], axis=1)[:, -WINDOW:].reshape(bs, WINDOW, N_KV_HEADS, HEAD_DIM)
    conv_s = conv_s.reshape(CONV_W - 1, bs, d).transpose(1, 0, 2)[None]
    h_s = h_s.reshape(1, bs, d)

    return (y_prompt, y_sample, conv_p, h_p, k_p, v_p, conv_s, h_s, k_s, v_s)
```

```python
import functools
import math

import jax
import jax.numpy as jnp
from jax import lax
from jax.experimental import pallas as pl
from jax.experimental.pallas import tpu as pltpu

D_MODEL = 1024
PAST_LEN = 16384
D_RNN = D_MODEL
RG_BLOCKS = 8
RG_BLOCK_W = D_RNN // RG_BLOCKS
CONV_W = 4
RG_C = 8.0
HEAD_DIM = 64
N_HEADS = D_MODEL // HEAD_DIM
N_KV_HEADS = 4
GROUP = N_HEADS // N_KV_HEADS
WINDOW = 128
ROPE_THETA = 10000.0
N_KEYS = 128
N_EXPERTS = N_KEYS * N_KEYS
PEER_HEADS = 8
PEER_TOPK = 16
D_KEY = 256
EPS = 1e-6

KV_DIM = N_KV_HEADS * HEAD_DIM
Q_DIM = N_HEADS * HEAD_DIM
PQ_DIM = PEER_HEADS * D_KEY

V7X_LANES = 128
V7X_SUBLANES = 8
V7X_VMEM_LIMIT_BYTES = 56 * 1024 * 1024

BF16 = jnp.bfloat16
F32 = jnp.float32


def _cparams(*sem, flags=None):
    return pltpu.CompilerParams(dimension_semantics=sem, vmem_limit_bytes=V7X_VMEM_LIMIT_BYTES, flags=flags)


def _rms(x, g):
    return x * lax.rsqrt(jnp.mean(x * x, axis=-1, keepdims=True) + EPS) * g


def _gelu_tanh(x):
    return 0.5 * x * (1.0 + jnp.tanh(math.sqrt(2.0 / math.pi) * (x + 0.044715 * (x * x * x))))


def _gelu_erf(x):
    return 0.5 * x * (1.0 + lax.erf(x * (1.0 / math.sqrt(2.0))))


def _sigmoid(x):
    return 1.0 / (1.0 + jnp.exp(-x))


def _rg_kernel(x_ref, conv0_ref, h0_ref, g_ref, win_ref, cw_ref, cb_ref, wa_ref, ba_ref,
               wx_ref, bx_ref, lam_ref, wout_ref,
               y_ref, conv_ref, hlast_ref,
               xbuf_ref, hcar_ref, *, stride, steps, pad):
    rows = stride * steps
    tail = (CONV_W - 1) * stride
    c = pl.program_id(1)

    @pl.when(c == 0)
    def _():
        xbuf_ref[0:pad, :] = conv0_ref[0]
        hcar_ref[...] = h0_ref[0]

    x = x_ref[0]
    xn = _rms(x, g_ref[...]).astype(BF16)
    proj = jnp.dot(xn, win_ref[...], preferred_element_type=F32)
    gate = _gelu_tanh(proj[:, :D_RNN])
    xr = proj[:, D_RNN:]
    xbuf_ref[pad:pad + rows, :] = xr

    y = cb_ref[...] + cw_ref[0:1, :] * xbuf_ref[pad - 3 * stride:pad - 3 * stride + rows, :]
    y = y + cw_ref[1:2, :] * xbuf_ref[pad - 2 * stride:pad - 2 * stride + rows, :]
    y = y + cw_ref[2:3, :] * xbuf_ref[pad - stride:pad - stride + rows, :]
    xc = y + cw_ref[3:4, :] * xr

    conv_ref[0] = xbuf_ref[pad + rows - tail:pad + rows, :]
    xbuf_ref[0:pad, :] = xbuf_ref[rows:rows + pad, :]

    xcb = xc.astype(BF16)
    rs, is_ = [], []
    for n in range(RG_BLOCKS):
        blk = xcb[:, n * RG_BLOCK_W:(n + 1) * RG_BLOCK_W]
        rs.append(jnp.dot(blk, wa_ref[n], preferred_element_type=F32))
        is_.append(jnp.dot(blk, wx_ref[n], preferred_element_type=F32))
    r = _sigmoid(jnp.concatenate(rs, axis=1) + ba_ref[...])
    i = _sigmoid(jnp.concatenate(is_, axis=1) + bx_ref[...])

    nlam = -lam_ref[...]
    softplus = jnp.maximum(nlam, 0.0) + jnp.log1p(jnp.exp(-jnp.abs(nlam)))
    log_a = -RG_C * r * softplus
    a = jnp.exp(log_a)
    mult = jnp.sqrt(-jnp.tanh(log_a) * (a * a + 1.0))
    b = mult * i * xc

    row = lax.broadcasted_iota(jnp.int32, (rows, D_RNN), 0)
    s = 1
    while s < steps:
        sh = s * stride
        a_sh = pltpu.roll(a, sh, axis=0)
        b_sh = pltpu.roll(b, sh, axis=0)
        m = row >= sh
        b = jnp.where(m, a * b_sh + b, b)
        a = jnp.where(m, a * a_sh, a)
        s *= 2
    if stride == 1:
        h = a * hcar_ref[...] + b
    else:
        hc = hcar_ref[...]
        h = a * jnp.concatenate([hc] * steps, axis=0) + b
    hl = h[rows - stride:rows, :]
    hcar_ref[...] = hl
    hlast_ref[0] = hl

    hg = (h * gate).astype(BF16)
    y_ref[0] = x + jnp.dot(hg, wout_ref[...], preferred_element_type=F32)


def _rg_layer(x3, conv0, h0, w, *, stride, steps):
    g_, r_, d_ = x3.shape
    rows = stride * steps
    assert r_ % rows == 0
    pad = conv0.shape[1]
    tail = (CONV_W - 1) * stride
    nchunks = r_ // rows
    const2 = lambda g, c: (0, 0)
    const3 = lambda g, c: (0, 0, 0)
    kern = functools.partial(_rg_kernel, stride=stride, steps=steps, pad=pad)
    return pl.pallas_call(
        kern,
        grid=(g_, nchunks),
        in_specs=[
            pl.BlockSpec((1, rows, d_), lambda g, c: (g, c, 0)),
            pl.BlockSpec((1, pad, d_), lambda g, c: (g, 0, 0)),
            pl.BlockSpec((1, stride, d_), lambda g, c: (g, 0, 0)),
            pl.BlockSpec((1, d_), const2),
            pl.BlockSpec((d_, 2 * D_RNN), const2),
            pl.BlockSpec((CONV_W, D_RNN), const2),
            pl.BlockSpec((1, D_RNN), const2),
            pl.BlockSpec((RG_BLOCKS, RG_BLOCK_W, RG_BLOCK_W), const3),
            pl.BlockSpec((1, D_RNN), const2),
            pl.BlockSpec((RG_BLOCKS, RG_BLOCK_W, RG_BLOCK_W), const3),
            pl.BlockSpec((1, D_RNN), const2),
            pl.BlockSpec((1, D_RNN), const2),
            pl.BlockSpec((D_RNN, d_), const2),
        ],
        out_specs=[
            pl.BlockSpec((1, rows, d_), lambda g, c: (g, c, 0)),
            pl.BlockSpec((1, tail, D_RNN), lambda g, c: (g, 0, 0)),
            pl.BlockSpec((1, stride, D_RNN), lambda g, c: (g, 0, 0)),
        ],
        out_shape=[
            jax.ShapeDtypeStruct((g_, r_, d_), F32),
            jax.ShapeDtypeStruct((g_, tail, D_RNN), F32),
            jax.ShapeDtypeStruct((g_, stride, D_RNN), F32),
        ],
        scratch_shapes=[
            pltpu.VMEM((pad + rows, D_RNN), F32),
            pltpu.VMEM((stride, D_RNN), F32),
        ],
        compiler_params=_cparams("arbitrary", "arbitrary"),
        name="rg_layer",
    )(x3, conv0, h0, w["g"], w["w_in"], w["conv_w"], w["conv_b"], w["wa"], w["ba"],
      w["wx"], w["bx"], w["lam"], w["w_out"])


def _ce(v, i, j):
    hi = jnp.maximum(v[i], v[j])
    lo = jnp.minimum(v[i], v[j])
    v[i], v[j] = hi, lo


def _sort16(v):
    n = 16
    k = 2
    while k <= n:
        j = k // 2
        while j >= 1:
            for i in range(n):
                l = i ^ j
                if l > i:
                    if (i & k) == 0:
                        _ce(v, i, l)
                    else:
                        _ce(v, l, i)
            j //= 2
        k *= 2


def _merge_top16(a, b):
    v = [jnp.maximum(a[k], b[15 - k]) for k in range(16)]
    j = 8
    while j >= 1:
        for i in range(16):
            l = i ^ j
            if l > i:
                _ce(v, i, l)
        j //= 2
    return v


def _top16(vs):
    groups = []
    for g in range(len(vs) // 16):
        lst = list(vs[16 * g:16 * g + 16])
        _sort16(lst)
        groups.append(lst)
    while len(groups) > 1:
        groups = [_merge_top16(groups[i], groups[i + 1]) for i in range(0, len(groups), 2)]
    return groups[0]


def _kth_product(sv0, sv1, mul):
    cands = [mul(sv0[0], sv1[j]) for j in range(PEER_TOPK)]
    for i in range(1, PEER_TOPK):
        for j in range(PEER_TOPK // (i + 1)):
            cands.append(mul(sv0[i], sv1[j]))
    padded = list(cands)
    while len(padded) % 16:
        padded.append(jnp.full_like(cands[0], -1.0))
    return _top16(padded)[PEER_TOPK - 1], cands


def _peer_select(sc_ref, lt):
    sv = []
    for p in range(2):
        vs = [sc_ref[p, lt, pl.ds(n, PEER_HEADS, stride=N_KEYS), :] for n in range(N_KEYS)]
        sv.append(_top16(vs))
    sv0, sv1 = sv
    theta, cands = _kth_product(sv0, sv1, lambda a, b: a * b)
    z = jnp.zeros_like(theta)
    for cnd in cands:
        z = z + jnp.where(cnd >= theta, cnd, 0.0)
    scale = 0.5 / z
    sv0n = [(v * scale).astype(BF16) for v in sv0]
    sv1b = [v.astype(BF16) for v in sv1]
    theta_n, _ = _kth_product(sv0n, sv1b, lambda a, b: (a * b).astype(F32))
    return scale, theta_n


def _peer_front_kernel(*refs, tm, with_attn):
    if with_attn:
        (x_ref, o_ref, wo_ref, g_ref, wqt_ref, sk_ref,
         x1_ref, xnt_ref, t0_ref, t1_ref, th_ref, sc_ref) = refs
        x1 = x_ref[...] + jnp.dot(o_ref[...].astype(BF16), wo_ref[...], preferred_element_type=F32)
        x1_ref[...] = x1
    else:
        (x_ref, g_ref, wqt_ref, sk_ref, xnt_ref, t0_ref, t1_ref, th_ref, sc_ref) = refs
        x1 = x_ref[...]
    xn = _rms(x1, g_ref[...])
    xnt = xn.T.astype(BF16)
    xnt_ref[...] = xnt
    qt = jnp.dot(wqt_ref[...], xnt, preferred_element_type=F32).astype(BF16)
    for p in range(2):
        for h in range(PEER_HEADS):
            base = p * (PQ_DIM // 2) + h * (D_KEY // 2)
            s = jnp.dot(sk_ref[p], qt[base:base + D_KEY // 2, :], preferred_element_type=F32)
            t = jnp.exp(s - jnp.max(s, axis=0, keepdims=True))
            if p == 1:
                t1_ref[h * N_KEYS:(h + 1) * N_KEYS, :] = t.astype(BF16)
                t = t.astype(BF16).astype(F32)
            for lt in range(tm // V7X_LANES):
                sc_ref[p, lt, h * N_KEYS:(h + 1) * N_KEYS, :] = t[:, lt * V7X_LANES:(lt + 1) * V7X_LANES]
    for lt in range(tm // V7X_LANES):
        lanes = pl.ds(lt * V7X_LANES, V7X_LANES)
        scale, theta = _peer_select(sc_ref, lt)
        th_ref[:, lanes] = theta
        for n in range(N_KEYS):
            t0_ref[n * PEER_HEADS:(n + 1) * PEER_HEADS, lanes] = (
                sc_ref[0, lt, pl.ds(n, PEER_HEADS, stride=N_KEYS), :] * scale).astype(BF16).astype(F32)


def _peer_front(x, w, attn=None, *, tm):
    n_, d_ = x.shape
    assert n_ % tm == 0
    with_attn = attn is not None
    const2 = lambda i: (0, 0)
    const3 = lambda i: (0, 0, 0)
    tok = pl.BlockSpec((tm, d_), lambda i: (i, 0))
    in_specs = [tok]
    args = [x]
    if with_attn:
        in_specs += [pl.BlockSpec((tm, Q_DIM), lambda i: (i, 0)), pl.BlockSpec((Q_DIM, d_), const2)]
        args += list(attn)
    in_specs += [
        pl.BlockSpec((1, d_), const2),
        pl.BlockSpec((PQ_DIM, d_), const2),
        pl.BlockSpec((2, N_KEYS, D_KEY // 2), const3),
    ]
    args += [w["g"], w["wqt"], w["sk"]]
    feat = lambda rows: pl.BlockSpec((rows, tm), lambda i: (0, i))
    out_specs = [feat(d_), feat(PEER_HEADS * N_KEYS), feat(PEER_HEADS * N_KEYS), feat(PEER_HEADS)]
    out_shape = [
        jax.ShapeDtypeStruct((d_, n_), BF16),
        jax.ShapeDtypeStruct((PEER_HEADS * N_KEYS, n_), F32),
        jax.ShapeDtypeStruct((PEER_HEADS * N_KEYS, n_), BF16),
        jax.ShapeDtypeStruct((PEER_HEADS, n_), F32),
    ]
    if with_attn:
        out_specs = [tok] + out_specs
        out_shape = [jax.ShapeDtypeStruct((n_, d_), F32)] + out_shape
    outs = pl.pallas_call(
        functools.partial(_peer_front_kernel, tm=tm, with_attn=with_attn),
        grid=(n_ // tm,),
        in_specs=in_specs,
        out_specs=out_specs,
        out_shape=out_shape,
        scratch_shapes=[pltpu.VMEM((2, tm // V7X_LANES, PEER_HEADS * N_KEYS, V7X_LANES), F32)],
        compiler_params=_cparams("arbitrary"),
        name="peer_front_attn" if with_attn else "peer_front",
    )(*args)
    if with_attn:
        return outs[0], outs[1:]
    return x, outs


def _peer_weight_tile(lt, keys, ht_ref, pt_ref, t0_ref, t1_ref, th_ref):
    rb = 2 * V7X_SUBLANES
    lanes = slice(lt * V7X_LANES, (lt + 1) * V7X_LANES)
    bcast = lambda row: jnp.broadcast_to(row, (rb, V7X_LANES)).astype(BF16)
    th = [bcast(th_ref[h:h + 1, lanes]) for h in range(PEER_HEADS)]
    zero = jnp.zeros((rb, V7X_LANES), BF16)
    for al in keys:
        ta = [bcast(t0_ref[al * PEER_HEADS + h:al * PEER_HEADS + h + 1, lanes]) for h in range(PEER_HEADS)]
        for bg in range(N_KEYS // rb):
            w = None
            for h in range(PEER_HEADS):
                prod = ta[h] * t1_ref[h * N_KEYS + bg * rb:h * N_KEYS + (bg + 1) * rb, lanes]
                wh = jnp.where(prod >= th[h], prod, zero)
                w = wh if w is None else w + wh
            r0 = al * N_KEYS + bg * rb
            hv = ht_ref[r0:r0 + rb, lanes]
            gl = hv * (1.0 + lax.erf(hv * (1.0 / math.sqrt(2.0))))
            pt_ref[r0:r0 + rb, lanes] = gl.astype(BF16) * w


def _peer_half_step(xnt_ref, u_ref, u_rows, vt_ref, vt_cols, acc_ref, ht_w, pt_r, ht_r, pt_w,
                    t0_ref, t1_ref, th_ref, *, eb, tm):
    mxu_w = 2 * V7X_LANES
    pieces = []
    for c in range(tm // mxu_w):
        cols = slice(c * mxu_w, (c + 1) * mxu_w)

        def scores(cols=cols):
            ht_w[:, cols] = jnp.dot(u_ref[u_rows, :], xnt_ref[:, cols], preferred_element_type=F32)

        def values(cols=cols):
            acc_ref[:, cols] += jnp.dot(vt_ref[:, vt_cols], pt_r[:, cols], preferred_element_type=F32)

        pieces += [scores, values]
    for lt in range(tm // V7X_LANES):
        _peer_weight_tile(lt, range(eb // N_KEYS), ht_r, pt_w, t0_ref, t1_ref, th_ref)
        pieces[lt]()


def _peer_dense_kernel(xnt_ref, u_ref, vt_ref, t0a_ref, t1a_ref, tha_ref, t0b_ref, t1b_ref, thb_ref,
                       x1_ref, gf_ref, out_ref, acc_ref, ht0_ref, ht1_ref, pt0_ref, pt1_ref,
                       *, eb, ne, final_norm):
    g = pl.program_id(0)
    tm = acc_ref.shape[1]
    fv = 2 * g - 2

    @pl.when(g == 0)
    def _():
        ht1_ref[...] = jnp.zeros_like(ht1_ref)
        pt0_ref[...] = jnp.zeros_like(pt0_ref)

    @pl.when((g == 0) | (lax.rem(fv, ne) == 0))
    def _():
        acc_ref[...] = jnp.zeros_like(acc_ref)

    _peer_half_step(xnt_ref, u_ref, slice(0, eb), vt_ref, slice(0, eb), acc_ref,
                    ht0_ref, pt0_ref, ht1_ref, pt1_ref, t0a_ref, t1a_ref, tha_ref, eb=eb, tm=tm)
    _peer_half_step(xnt_ref, u_ref, slice(eb, 2 * eb), vt_ref, slice(eb, 2 * eb), acc_ref,
                    ht1_ref, pt1_ref, ht0_ref, pt0_ref, t0b_ref, t1b_ref, thb_ref, eb=eb, tm=tm)

    @pl.when((g > 0) & (lax.rem(fv + 1, ne) == ne - 1))
    def _():
        y = x1_ref[...] + acc_ref[...].T
        if final_norm:
            y = _rms(y, gf_ref[...])
        out_ref[...] = y


def _peer_dense(x1, front, w, gf, *, tm, eb, final_norm):
    xnt, t0, t1, th = front
    n_, d_ = x1.shape
    ne = N_EXPERTS // eb
    assert n_ % tm == 0 and N_EXPERTS % eb == 0 and eb % N_KEYS == 0 and ne % 2 == 0
    assert tm % (2 * V7X_LANES) == 0
    nblocks = (n_ // tm) * ne
    last = nblocks - 1
    tok_mm1 = lambda g: jnp.minimum(2 * g, last) // ne
    tok_a = lambda g: jnp.clip(2 * g - 1, 0, last) // ne
    tok_v = lambda g: jnp.maximum(2 * g - 2, 0) // ne
    blk_a = lambda g: jnp.clip(2 * g - 1, 0, last) % ne
    blk_b = lambda g: jnp.minimum(2 * g, last) % ne
    pair_mm1 = lambda g: (jnp.minimum(2 * g, last) % ne) // 2
    pair_v = lambda g: (jnp.maximum(2 * g - 2, 0) % ne) // 2
    feat = lambda rows, tok: pl.BlockSpec((rows, tm), lambda g: (0, tok(g)))
    nk = PEER_HEADS * N_KEYS
    t0_rows = (eb // N_KEYS) * PEER_HEADS
    return pl.pallas_call(
        functools.partial(_peer_dense_kernel, eb=eb, ne=ne, final_norm=final_norm),
        grid=(nblocks // 2 + 1,),
        in_specs=[
            feat(d_, tok_mm1),
            pl.BlockSpec((2 * eb, d_), lambda g: (pair_mm1(g), 0)),
            pl.BlockSpec((d_, 2 * eb), lambda g: (0, pair_v(g))),
            pl.BlockSpec((t0_rows, tm), lambda g: (blk_a(g), tok_a(g))), feat(nk, tok_a), feat(PEER_HEADS, tok_a),
            pl.BlockSpec((t0_rows, tm), lambda g: (blk_b(g), tok_mm1(g))), feat(nk, tok_mm1),
            feat(PEER_HEADS, tok_mm1),
            pl.BlockSpec((tm, d_), lambda g: (tok_v(g), 0)),
            pl.BlockSpec((1, d_), lambda g: (0, 0)),
        ],
        out_specs=pl.BlockSpec((tm, d_), lambda g: (tok_v(g), 0)),
        out_shape=jax.ShapeDtypeStruct((n_, d_), F32),
        scratch_shapes=[
            pltpu.VMEM((d_, tm), F32),
            pltpu.VMEM((eb, tm), F32), pltpu.VMEM((eb, tm), F32),
            pltpu.VMEM((eb, tm), BF16), pltpu.VMEM((eb, tm), BF16),
        ],
        compiler_params=_cparams("arbitrary"),
        name="peer_dense",
    )(xnt, w["u"], w["vt"], t0, t1, th, t0, t1, th, x1, gf)


def _rope(x, cos, sin_signed):
    half = HEAD_DIM // 2
    lane = lax.broadcasted_iota(jnp.int32, (1, V7X_LANES), 1)
    first = (lane % HEAD_DIM) < half
    outs = []
    for t in range(x.shape[1] // V7X_LANES):
        xt = x[:, t * V7X_LANES:(t + 1) * V7X_LANES]
        swapped = jnp.where(first, pltpu.roll(xt, V7X_LANES - half, axis=1), pltpu.roll(xt, half, axis=1))
        outs.append(xt * cos + swapped * sin_signed)
    return jnp.concatenate(outs, axis=1)


def _qkv_kernel(x_ref, gkv_ref, wkv_ref, gq_ref, wq_ref, cos_ref, sin_ref, q_ref, k_ref, v_ref):
    x = x_ref[...]
    cos = cos_ref[...]
    sin = sin_ref[...]
    kv = jnp.dot(_rms(x, gkv_ref[...]).astype(BF16), wkv_ref[...], preferred_element_type=F32)
    k_ref[...] = _rope(kv[:, :KV_DIM], cos, sin)
    v_ref[...] = kv[:, KV_DIM:]
    q = jnp.dot(_rms(x, gq_ref[...]).astype(BF16), wq_ref[...], preferred_element_type=F32)
    q_ref[...] = (_rope(q, cos, sin) * (HEAD_DIM ** -0.5)).astype(q_ref.dtype)


def _qkv(x, w, cos, sin, *, tm, table_blocks, q_dtype):
    n_, d_ = x.shape
    assert n_ % tm == 0
    const2 = lambda i: (0, 0)
    tok = lambda cols: pl.BlockSpec((tm, cols), lambda i: (i, 0))
    tab = pl.BlockSpec((tm, V7X_LANES), lambda i: (i % table_blocks, 0))
    return pl.pallas_call(
        _qkv_kernel,
        grid=(n_ // tm,),
        in_specs=[tok(d_), pl.BlockSpec((1, d_), const2), pl.BlockSpec((d_, 2 * KV_DIM), const2),
                  pl.BlockSpec((1, d_), const2), pl.BlockSpec((d_, Q_DIM), const2), tab, tab],
        out_specs=[tok(Q_DIM), tok(KV_DIM), tok(KV_DIM)],
        out_shape=[jax.ShapeDtypeStruct((n_, Q_DIM), q_dtype),
                   jax.ShapeDtypeStruct((n_, KV_DIM), F32),
                   jax.ShapeDtypeStruct((n_, KV_DIM), F32)],
        compiler_params=_cparams("arbitrary"),
        name="qkv_rope",
    )(x, w["g_kv"], w["w_kv"], w["g_q"], w["w_q"], cos, sin)


def _sink_attend(s, mask, sink, v):
    s = jnp.where(mask, s, -jnp.inf)
    m = jnp.maximum(jnp.max(s, axis=-1, keepdims=True), sink)
    p = jnp.exp(s - m)
    denom = jnp.sum(p, axis=-1, keepdims=True) + jnp.exp(sink - m)
    return jnp.dot(p.astype(BF16), v, preferred_element_type=F32) / denom


def _swa_prompt_kernel(sink_ref, q_ref, kp_ref, kc_ref, vp_ref, vc_ref, o_ref):
    n = pl.program_id(1)
    c = WINDOW
    k2 = jnp.concatenate([kp_ref[...], kc_ref[...]], axis=0).astype(BF16)
    v2 = jnp.concatenate([vp_ref[...], vc_ref[...]], axis=0).astype(BF16)
    qi = lax.broadcasted_iota(jnp.int32, (c, 2 * c), 0)
    kj = lax.broadcasted_iota(jnp.int32, (c, 2 * c), 1)
    mask = (kj > qi) & (kj <= qi + c) & ((kj >= c) | (n > 0))
    outs = []
    for hd in range(N_HEADS):
        kvh = hd // GROUP
        q = q_ref[:, hd * HEAD_DIM:(hd + 1) * HEAD_DIM]
        kk = k2[:, kvh * HEAD_DIM:(kvh + 1) * HEAD_DIM]
        s = lax.dot_general(q, kk, (((1,), (1,)), ((), ())), preferred_element_type=F32)
        outs.append(_sink_attend(s, mask, sink_ref[hd], v2[:, kvh * HEAD_DIM:(kvh + 1) * HEAD_DIM]))
    o_ref[...] = jnp.concatenate(outs, axis=1).astype(BF16)


def _swa_prompt(q, k, v, sinks, *, batch, seq):
    nb = seq // WINDOW
    cur = lambda cols: pl.BlockSpec((WINDOW, cols), lambda b, n: (b * nb + n, 0))
    prev = lambda cols: pl.BlockSpec((WINDOW, cols), lambda b, n: (b * nb + jnp.maximum(n - 1, 0), 0))
    return pl.pallas_call(
        _swa_prompt_kernel,
        grid=(batch, nb),
        in_specs=[pl.BlockSpec(memory_space=pltpu.SMEM),
                  cur(Q_DIM), prev(KV_DIM), cur(KV_DIM), prev(KV_DIM), cur(KV_DIM)],
        out_specs=cur(Q_DIM),
        out_shape=jax.ShapeDtypeStruct((batch * seq, Q_DIM), BF16),
        compiler_params=_cparams("arbitrary", "arbitrary"),
        name="swa_prompt",
    )(sinks, q, k, k, v, v)


def _swa_sample_kernel(sink_ref, q_ref, kc_ref, kn_ref, vc_ref, vn_ref, o_ref, k2_ref, v2_ref, *, nq):
    c = WINDOW
    for src_c, src_n, dst in ((kc_ref, kn_ref, k2_ref), (vc_ref, vn_ref, v2_ref)):
        dst[0:c, :] = src_c[0]
        dst[c:2 * c, :] = jnp.zeros((c, KV_DIM), F32)
        dst[c:c + nq, :] = src_n[0]
    k2 = k2_ref[...].astype(BF16)
    v2 = v2_ref[...].astype(BF16)
    q = q_ref[0]
    rows = GROUP * nq
    qi = lax.broadcasted_iota(jnp.int32, (rows, 2 * c), 0) % nq
    kj = lax.broadcasted_iota(jnp.int32, (rows, 2 * c), 1)
    mask = ((kj < c) & (kj > qi)) | ((kj >= c) & (kj - c <= qi))
    outs = [None] * N_HEADS
    for kvh in range(N_KV_HEADS):
        heads = range(kvh * GROUP, (kvh + 1) * GROUP)
        qs = jnp.concatenate([q[:, hd * HEAD_DIM:(hd + 1) * HEAD_DIM] for hd in heads], axis=0).astype(BF16)
        sink = jnp.concatenate([jnp.full((nq, 1), sink_ref[hd], F32) for hd in heads], axis=0)
        s = lax.dot_general(qs, k2[:, kvh * HEAD_DIM:(kvh + 1) * HEAD_DIM], (((1,), (1,)), ((), ())),
                            preferred_element_type=F32)
        o = _sink_attend(s, mask, sink, v2[:, kvh * HEAD_DIM:(kvh + 1) * HEAD_DIM])
        for g, hd in enumerate(heads):
            outs[hd] = o[g * nq:(g + 1) * nq, :]
    o_ref[0] = jnp.concatenate(outs, axis=1)


def _swa_sample(q, kc, kn, vc, vn, sinks):
    b_, nq, _ = q.shape
    assert nq <= WINDOW
    blk = lambda rows, cols: pl.BlockSpec((1, rows, cols), lambda b: (b, 0, 0))
    return pl.pallas_call(
        functools.partial(_swa_sample_kernel, nq=nq),
        grid=(b_,),
        in_specs=[pl.BlockSpec(memory_space=pltpu.SMEM),
                  blk(nq, Q_DIM), blk(WINDOW, KV_DIM), blk(nq, KV_DIM), blk(WINDOW, KV_DIM), blk(nq, KV_DIM)],
        out_specs=blk(nq, Q_DIM),
        out_shape=jax.ShapeDtypeStruct((b_, nq, Q_DIM), F32),
        scratch_shapes=[pltpu.VMEM((2 * WINDOW, KV_DIM), F32), pltpu.VMEM((2 * WINDOW, KV_DIM), F32)],
        compiler_params=_cparams("arbitrary"),
        name="swa_sample",
    )(sinks, q, kc, kn, vc, vn)


def _rope_tables(pos):
    half = HEAD_DIM // 2
    inv = ROPE_THETA ** (-jnp.arange(half, dtype=F32) / half)
    ang = pos.astype(F32)[:, None] * inv[None, :]
    cos = jnp.cos(ang)
    sin = jnp.sin(ang)
    reps = V7X_LANES // HEAD_DIM
    cos_t = jnp.tile(jnp.concatenate([cos, cos], axis=1), (1, reps))
    sin_t = jnp.tile(jnp.concatenate([-sin, sin], axis=1), (1, reps))
    return cos_t, sin_t


def _pick(n, cands):
    for c in cands:
        if n % c == 0:
            return c
    raise ValueError(f"no tile for {n}")


def _peer(x1, pw, gf, attn=None, *, final_norm):
    n_ = x1.shape[0]
    x1, front = _peer_front(x1, pw, attn, tm=_pick(n_, (256, 128)))
    return _peer_dense(x1, front, pw, gf, tm=_pick(n_, (512, 256, 128)), eb=1024, final_norm=final_norm)


def kernel(x_prompt, x_sample, state_conv, state_h, cache_k, cache_v, rg_norm, rg_w_in, rg_conv_w,
           rg_conv_b, rg_wa, rg_ba, rg_wx, rg_bx, rg_lambda, rg_w_out, kv_norm, w_kv, attn_norm, w_q,
           sinks, w_o, ffn_norm, peer_wq, peer_subkeys, peer_u, peer_v, final_norm):
    d = D_MODEL
    row = lambda a: a.reshape(1, -1).astype(F32)
    rgw = dict(g=row(rg_norm[0]), w_in=rg_w_in[0].astype(BF16), conv_w=rg_conv_w[0],
               conv_b=row(rg_conv_b[0]), wa=rg_wa[0].astype(BF16), ba=row(rg_ba[0]),
               wx=rg_wx[0].astype(BF16), bx=row(rg_bx[0]), lam=row(rg_lambda[0]),
               w_out=rg_w_out[0].astype(BF16))
    peer_w = []
    for l in range(2):
        wq = peer_wq[l].reshape(d, PEER_HEADS, 2, D_KEY // 2).transpose(2, 1, 3, 0).reshape(PQ_DIM, d)
        peer_w.append(dict(g=row(ffn_norm[l]), wqt=wq.astype(BF16), sk=peer_subkeys[l].astype(BF16),
                           u=peer_u[l].astype(BF16), vt=peer_v[l].T.astype(BF16)))
    attw = dict(g_kv=row(kv_norm), w_kv=w_kv.astype(BF16), g_q=row(attn_norm[0]), w_q=w_q[0].astype(BF16))
    wo = w_o[0].astype(BF16)
    sink = sinks[0].astype(F32)
    gf = row(final_norm)

    bp, tp, _ = x_prompt.shape
    steps_p = _pick(tp, (256, 128, 64, 32, 16, 8))
    pad_p = V7X_SUBLANES
    x1, conv_p, h_p = _rg_layer(x_prompt, jnp.zeros((bp, pad_p, d), F32), jnp.zeros((bp, 1, d), F32),
                                rgw, stride=1, steps=steps_p)
    xp = _peer(x1.reshape(bp * tp, d), peer_w[0], gf, final_norm=False)
    cos_p, sin_p = _rope_tables(jnp.arange(tp))
    tm_p = _pick(tp, (512, 256, 128))
    q, k, v = _qkv(xp, attw, cos_p, sin_p, tm=tm_p, table_blocks=tp // tm_p, q_dtype=BF16)
    o = _swa_prompt(q, k, v, sink, batch=bp, seq=tp)
    y_prompt = _peer(xp, peer_w[1], gf, attn=(o, wo), final_norm=True).reshape(bp, tp, d)
    k_p = k.reshape(bp, tp, N_KV_HEADS, HEAD_DIM)[:, -WINDOW:]
    v_p = v.reshape(bp, tp, N_KV_HEADS, HEAD_DIM)[:, -WINDOW:]
    conv_p = conv_p[None]
    h_p = h_p.reshape(1, bp, d)

    bs, ts, _ = x_sample.shape
    xs = x_sample.transpose(1, 0, 2).reshape(1, ts * bs, d)
    conv0 = state_conv[0].transpose(1, 0, 2).reshape(1, (CONV_W - 1) * bs, d)
    x1s, conv_s, h_s = _rg_layer(xs, conv0, state_h[0][None], rgw, stride=bs, steps=ts)
    xs1 = _peer(x1s.reshape(ts * bs, d), peer_w[0], gf, final_norm=False)
    xs1 = xs1.reshape(ts, bs, d).transpose(1, 0, 2).reshape(bs * ts, d)
    cos_s, sin_s = _rope_tables(PAST_LEN + jnp.arange(ts))
    cos_s = jnp.tile(cos_s, (bs, 1))
    sin_s = jnp.tile(sin_s, (bs, 1))
    tm_s = _pick(bs * ts, (512, 256, 128, 64, 32, 16, 8))
    qs, ks, vs = _qkv(xs1, attw, cos_s, sin_s, tm=tm_s, table_blocks=(bs * ts) // tm_s, q_dtype=F32)
    ck = cache_k.reshape(bs, WINDOW, KV_DIM)
    cv = cache_v.reshape(bs, WINDOW, KV_DIM)
    kn = ks.reshape(bs, ts, KV_DIM)
    vn = vs.reshape(bs, ts, KV_DIM)
    os_ = _swa_sample(qs.reshape(bs, ts, Q_DIM), ck, kn, cv, vn, sink)
    y_sample = _peer(xs1, peer_w[1], gf, attn=(os_.reshape(bs * ts, Q_DIM), wo),
                     final_norm=True).reshape(bs, ts, d)
    k_s = jnp.concatenate([ck, kn], axis=1)[:, -WINDOW:].reshape(bs, WINDOW, N_KV_HEADS, HEAD_DIM)
    v_s = jnp.concatenate([cv, vn], axis=1)[:, -WINDOW:].reshape(bs, WINDOW, N_KV_HEADS, HEAD_DIM)
    conv_s = conv_s.reshape(CONV_W - 1, bs, d).transpose(1, 0, 2)[None]
    h_s = h_s.reshape(1, bs, d)

    return (y_prompt, y_sample, conv_p, h_p, k_p, v_p, conv_s, h_s, k_s, v_s)
```

```python
import functools
import math

import jax
import jax.numpy as jnp
from jax import lax
from jax.experimental import pallas as pl
from jax.experimental.pallas import tpu as pltpu

D_MODEL = 1024
PAST_LEN = 16384
D_RNN = D_MODEL
RG_BLOCKS = 8
RG_BLOCK_W = D_RNN // RG_BLOCKS
CONV_W = 4
RG_C = 8.0
HEAD_DIM = 64
N_HEADS = D_MODEL // HEAD_DIM
N_KV_HEADS = 4
GROUP = N_HEADS // N_KV_HEADS
WINDOW = 128
ROPE_THETA = 10000.0
N_KEYS = 128
N_EXPERTS = N_KEYS * N_KEYS
PEER_HEADS = 8
PEER_TOPK = 16
D_KEY = 256
EPS = 1e-6

KV_DIM = N_KV_HEADS * HEAD_DIM
Q_DIM = N_HEADS * HEAD_DIM
PQ_DIM = PEER_HEADS * D_KEY

V7X_LANES = 128
V7X_SUBLANES = 8
V7X_VMEM_LIMIT_BYTES = 56 * 1024 * 1024

BF16 = jnp.bfloat16
F32 = jnp.float32


def _cparams(*sem, flags=None):
    return pltpu.CompilerParams(dimension_semantics=sem, vmem_limit_bytes=V7X_VMEM_LIMIT_BYTES, flags=flags)


def _rms(x, g):
    return x * lax.rsqrt(jnp.mean(x * x, axis=-1, keepdims=True) + EPS) * g


def _gelu_tanh(x):
    return 0.5 * x * (1.0 + jnp.tanh(math.sqrt(2.0 / math.pi) * (x + 0.044715 * (x * x * x))))


def _gelu_erf(x):
    return 0.5 * x * (1.0 + lax.erf(x * (1.0 / math.sqrt(2.0))))


def _sigmoid(x):
    return 1.0 / (1.0 + jnp.exp(-x))


def _rg_kernel(x_ref, conv0_ref, h0_ref, g_ref, win_ref, cw_ref, cb_ref, wa_ref, ba_ref,
               wx_ref, bx_ref, lam_ref, wout_ref,
               y_ref, conv_ref, hlast_ref,
               xbuf_ref, hcar_ref, *, stride, steps, pad):
    rows = stride * steps
    tail = (CONV_W - 1) * stride
    c = pl.program_id(1)

    @pl.when(c == 0)
    def _():
        xbuf_ref[0:pad, :] = conv0_ref[0]
        hcar_ref[...] = h0_ref[0]

    x = x_ref[0]
    xn = _rms(x, g_ref[...]).astype(BF16)
    proj = jnp.dot(xn, win_ref[...], preferred_element_type=F32)
    gate = _gelu_tanh(proj[:, :D_RNN])
    xr = proj[:, D_RNN:]
    xbuf_ref[pad:pad + rows, :] = xr

    y = cb_ref[...] + cw_ref[0:1, :] * xbuf_ref[pad - 3 * stride:pad - 3 * stride + rows, :]
    y = y + cw_ref[1:2, :] * xbuf_ref[pad - 2 * stride:pad - 2 * stride + rows, :]
    y = y + cw_ref[2:3, :] * xbuf_ref[pad - stride:pad - stride + rows, :]
    xc = y + cw_ref[3:4, :] * xr

    conv_ref[0] = xbuf_ref[pad + rows - tail:pad + rows, :]
    xbuf_ref[0:pad, :] = xbuf_ref[rows:rows + pad, :]

    xcb = xc.astype(BF16)
    rs, is_ = [], []
    for n in range(RG_BLOCKS):
        blk = xcb[:, n * RG_BLOCK_W:(n + 1) * RG_BLOCK_W]
        rs.append(jnp.dot(blk, wa_ref[n], preferred_element_type=F32))
        is_.append(jnp.dot(blk, wx_ref[n], preferred_element_type=F32))
    r = _sigmoid(jnp.concatenate(rs, axis=1) + ba_ref[...])
    i = _sigmoid(jnp.concatenate(is_, axis=1) + bx_ref[...])

    nlam = -lam_ref[...]
    softplus = jnp.maximum(nlam, 0.0) + jnp.log1p(jnp.exp(-jnp.abs(nlam)))
    log_a = -RG_C * r * softplus
    a = jnp.exp(log_a)
    mult = jnp.sqrt(-jnp.tanh(log_a) * (a * a + 1.0))
    b = mult * i * xc

    group = V7X_SUBLANES if (stride == 1 and steps % V7X_SUBLANES == 0) else steps
    row = lax.broadcasted_iota(jnp.int32, (rows, D_RNN), 0) % (group * stride)
    s = 1
    while s < group:
        sh = s * stride
        a_sh = pltpu.roll(a, sh, axis=0)
        b_sh = pltpu.roll(b, sh, axis=0)
        m = row >= sh
        b = jnp.where(m, a * b_sh + b, b)
        a = jnp.where(m, a * a_sh, a)
        s *= 2
    grows = group * stride
    carry = hcar_ref[...]
    hs = []
    for g0 in range(0, rows, grows):
        cg = carry if stride == 1 else jnp.concatenate([carry] * group, axis=0)
        hg_ = a[g0:g0 + grows, :] * cg + b[g0:g0 + grows, :]
        hs.append(hg_)
        carry = hg_[grows - stride:grows, :]
    h = hs[0] if len(hs) == 1 else jnp.concatenate(hs, axis=0)
    hl = carry
    hcar_ref[...] = hl
    hlast_ref[0] = hl

    hg = (h * gate).astype(BF16)
    y_ref[0] = x + jnp.dot(hg, wout_ref[...], preferred_element_type=F32)


def _rg_layer(x3, conv0, h0, w, *, stride, steps):
    g_, r_, d_ = x3.shape
    rows = stride * steps
    assert r_ % rows == 0
    pad = conv0.shape[1]
    tail = (CONV_W - 1) * stride
    nchunks = r_ // rows
    const2 = lambda g, c: (0, 0)
    const3 = lambda g, c: (0, 0, 0)
    kern = functools.partial(_rg_kernel, stride=stride, steps=steps, pad=pad)
    return pl.pallas_call(
        kern,
        grid=(g_, nchunks),
        in_specs=[
            pl.BlockSpec((1, rows, d_), lambda g, c: (g, c, 0)),
            pl.BlockSpec((1, pad, d_), lambda g, c: (g, 0, 0)),
            pl.BlockSpec((1, stride, d_), lambda g, c: (g, 0, 0)),
            pl.BlockSpec((1, d_), const2),
            pl.BlockSpec((d_, 2 * D_RNN), const2),
            pl.BlockSpec((CONV_W, D_RNN), const2),
            pl.BlockSpec((1, D_RNN), const2),
            pl.BlockSpec((RG_BLOCKS, RG_BLOCK_W, RG_BLOCK_W), const3),
            pl.BlockSpec((1, D_RNN), const2),
            pl.BlockSpec((RG_BLOCKS, RG_BLOCK_W, RG_BLOCK_W), const3),
            pl.BlockSpec((1, D_RNN), const2),
            pl.BlockSpec((1, D_RNN), const2),
            pl.BlockSpec((D_RNN, d_), const2),
        ],
        out_specs=[
            pl.BlockSpec((1, rows, d_), lambda g, c: (g, c, 0)),
            pl.BlockSpec((1, tail, D_RNN), lambda g, c: (g, 0, 0)),
            pl.BlockSpec((1, stride, D_RNN), lambda g, c: (g, 0, 0)),
        ],
        out_shape=[
            jax.ShapeDtypeStruct((g_, r_, d_), F32),
            jax.ShapeDtypeStruct((g_, tail, D_RNN), F32),
            jax.ShapeDtypeStruct((g_, stride, D_RNN), F32),
        ],
        scratch_shapes=[
            pltpu.VMEM((pad + rows, D_RNN), F32),
            pltpu.VMEM((stride, D_RNN), F32),
        ],
        compiler_params=_cparams("arbitrary", "arbitrary"),
        name="rg_layer",
    )(x3, conv0, h0, w["g"], w["w_in"], w["conv_w"], w["conv_b"], w["wa"], w["ba"],
      w["wx"], w["bx"], w["lam"], w["w_out"])


def _ce(v, i, j):
    hi = jnp.maximum(v[i], v[j])
    lo = jnp.minimum(v[i], v[j])
    v[i], v[j] = hi, lo


def _sort16(v):
    n = 16
    k = 2
    while k <= n:
        j = k // 2
        while j >= 1:
            for i in range(n):
                l = i ^ j
                if l > i:
                    if (i & k) == 0:
                        _ce(v, i, l)
                    else:
                        _ce(v, l, i)
            j //= 2
        k *= 2


def _merge_top16(a, b):
    v = [jnp.maximum(a[k], b[15 - k]) for k in range(16)]
    j = 8
    while j >= 1:
        for i in range(16):
            l = i ^ j
            if l > i:
                _ce(v, i, l)
        j //= 2
    return v


def _top16(vs):
    groups = []
    for g in range(len(vs) // 16):
        lst = list(vs[16 * g:16 * g + 16])
        _sort16(lst)
        groups.append(lst)
    while len(groups) > 1:
        groups = [_merge_top16(groups[i], groups[i + 1]) for i in range(0, len(groups), 2)]
    return groups[0]


def _kth_product(sv0, sv1, mul):
    cands = [mul(sv0[0], sv1[j]) for j in range(PEER_TOPK)]
    for i in range(1, PEER_TOPK):
        for j in range(PEER_TOPK // (i + 1)):
            cands.append(mul(sv0[i], sv1[j]))
    padded = list(cands)
    while len(padded) % 16:
        padded.append(jnp.full_like(cands[0], -1.0))
    return _top16(padded)[PEER_TOPK - 1], cands


def _peer_select(sc_ref, lt):
    sv0 = _top16([sc_ref[0, lt, n * PEER_HEADS:(n + 1) * PEER_HEADS, :] for n in range(N_KEYS)])
    sv1 = _top16([sc_ref[1, lt, pl.ds(n, PEER_HEADS, stride=N_KEYS), :] for n in range(N_KEYS)])
    theta, cands = _kth_product(sv0, sv1, lambda a, b: a * b)
    z = jnp.zeros_like(theta)
    for cnd in cands:
        z = z + jnp.where(cnd >= theta, cnd, 0.0)
    scale = 0.5 / z
    sv0n = [(v * scale).astype(BF16) for v in sv0]
    sv1b = [v.astype(BF16) for v in sv1]
    theta_n, _ = _kth_product(sv0n, sv1b, lambda a, b: (a * b).astype(F32))
    return scale, theta_n


def _peer_front_kernel(*refs, tm, with_attn):
    if with_attn:
        (x_ref, o_ref, wo_ref, g_ref, wqt_ref, skx_ref, sk_ref,
         x1_ref, xnt_ref, t0_ref, t1_ref, th_ref, sc_ref) = refs
        x1 = x_ref[...] + jnp.dot(o_ref[...].astype(BF16), wo_ref[...], preferred_element_type=F32)
        x1_ref[...] = x1
    else:
        (x_ref, g_ref, wqt_ref, skx_ref, sk_ref, xnt_ref, t0_ref, t1_ref, th_ref, sc_ref) = refs
        x1 = x_ref[...]
    nh = PEER_HEADS
    xn = _rms(x1, g_ref[...])
    xnt = xn.T.astype(BF16)
    xnt_ref[...] = xnt
    qt = jnp.dot(wqt_ref[...], xnt, preferred_element_type=F32).astype(BF16)
    s0 = jnp.dot(skx_ref[...], qt[0:PQ_DIM // 2, :], preferred_element_type=F32)
    mx = s0[0:nh, :]
    for n in range(1, N_KEYS):
        mx = jnp.maximum(mx, s0[n * nh:(n + 1) * nh, :])
    for n in range(N_KEYS):
        t = jnp.exp(s0[n * nh:(n + 1) * nh, :] - mx)
        for lt in range(tm // V7X_LANES):
            sc_ref[0, lt, n * nh:(n + 1) * nh, :] = t[:, lt * V7X_LANES:(lt + 1) * V7X_LANES]
    for h in range(nh):
        base = PQ_DIM // 2 + h * (D_KEY // 2)
        s = jnp.dot(sk_ref[...], qt[base:base + D_KEY // 2, :], preferred_element_type=F32)
        t = jnp.exp(s - jnp.max(s, axis=0, keepdims=True)).astype(BF16)
        t1_ref[h * N_KEYS:(h + 1) * N_KEYS, :] = t
        t = t.astype(F32)
        for lt in range(tm // V7X_LANES):
            sc_ref[1, lt, h * N_KEYS:(h + 1) * N_KEYS, :] = t[:, lt * V7X_LANES:(lt + 1) * V7X_LANES]
    for lt in range(tm // V7X_LANES):
        lanes = pl.ds(lt * V7X_LANES, V7X_LANES)
        scale, theta = _peer_select(sc_ref, lt)
        th_ref[:, lanes] = theta
        for n in range(N_KEYS):
            t0_ref[n * nh:(n + 1) * nh, lanes] = (
                sc_ref[0, lt, n * nh:(n + 1) * nh, :] * scale).astype(BF16).astype(F32)


def _peer_front(x, w, attn=None, *, tm):
    n_, d_ = x.shape
    assert n_ % tm == 0
    with_attn = attn is not None
    const2 = lambda i: (0, 0)
    const3 = lambda i: (0, 0, 0)
    tok = pl.BlockSpec((tm, d_), lambda i: (i, 0))
    in_specs = [tok]
    args = [x]
    if with_attn:
        in_specs += [pl.BlockSpec((tm, Q_DIM), lambda i: (i, 0)), pl.BlockSpec((Q_DIM, d_), const2)]
        args += list(attn)
    in_specs += [
        pl.BlockSpec((1, d_), const2),
        pl.BlockSpec((PQ_DIM, d_), const2),
        pl.BlockSpec((PEER_HEADS * N_KEYS, PQ_DIM // 2), const2),
        pl.BlockSpec((N_KEYS, D_KEY // 2), const2),
    ]
    args += [w["g"], w["wqt"], w["skx0"], w["sk1"]]
    feat = lambda rows: pl.BlockSpec((rows, tm), lambda i: (0, i))
    out_specs = [feat(d_), feat(PEER_HEADS * N_KEYS), feat(PEER_HEADS * N_KEYS), feat(PEER_HEADS)]
    out_shape = [
        jax.ShapeDtypeStruct((d_, n_), BF16),
        jax.ShapeDtypeStruct((PEER_HEADS * N_KEYS, n_), F32),
        jax.ShapeDtypeStruct((PEER_HEADS * N_KEYS, n_), BF16),
        jax.ShapeDtypeStruct((PEER_HEADS, n_), F32),
    ]
    if with_attn:
        out_specs = [tok] + out_specs
        out_shape = [jax.ShapeDtypeStruct((n_, d_), F32)] + out_shape
    outs = pl.pallas_call(
        functools.partial(_peer_front_kernel, tm=tm, with_attn=with_attn),
        grid=(n_ // tm,),
        in_specs=in_specs,
        out_specs=out_specs,
        out_shape=out_shape,
        scratch_shapes=[pltpu.VMEM((2, tm // V7X_LANES, PEER_HEADS * N_KEYS, V7X_LANES), F32)],
        compiler_params=_cparams("arbitrary"),
        name="peer_front_attn" if with_attn else "peer_front",
    )(*args)
    if with_attn:
        return outs[0], outs[1:]
    return x, outs


def _peer_weight_tile(lt, keys, ht_ref, pt_ref, t0_ref, t1_ref, th_ref):
    rb = 2 * V7X_SUBLANES
    lanes = slice(lt * V7X_LANES, (lt + 1) * V7X_LANES)
    bcast = lambda row: jnp.broadcast_to(row, (rb, V7X_LANES)).astype(BF16)
    th = [bcast(th_ref[h:h + 1, lanes]) for h in range(PEER_HEADS)]
    zero = jnp.zeros((rb, V7X_LANES), BF16)
    for al in keys:
        ta = [bcast(t0_ref[al * PEER_HEADS + h:al * PEER_HEADS + h + 1, lanes]) for h in range(PEER_HEADS)]
        for bg in range(N_KEYS // rb):
            w = None
            for h in range(PEER_HEADS):
                prod = ta[h] * t1_ref[h * N_KEYS + bg * rb:h * N_KEYS + (bg + 1) * rb, lanes]
                wh = jnp.where(prod >= th[h], prod, zero)
                w = wh if w is None else w + wh
            r0 = al * N_KEYS + bg * rb
            hv = ht_ref[r0:r0 + rb, lanes]
            gl = hv * (1.0 + lax.erf(hv * (1.0 / math.sqrt(2.0))))
            pt_ref[r0:r0 + rb, lanes] = gl.astype(BF16) * w


def _peer_half_step(xnt_ref, u_ref, u_rows, vt_ref, vt_cols, acc_ref, ht_w, pt_r, ht_r, pt_w,
                    t0_ref, t1_ref, th_ref, *, eb, tm):
    mxu_w = 2 * V7X_LANES
    pieces = []
    for c in range(tm // mxu_w):
        cols = slice(c * mxu_w, (c + 1) * mxu_w)

        def scores(cols=cols):
            ht_w[:, cols] = jnp.dot(u_ref[u_rows, :], xnt_ref[:, cols], preferred_element_type=F32)

        def values(cols=cols):
            acc_ref[:, cols] += jnp.dot(vt_ref[:, vt_cols], pt_r[:, cols], preferred_element_type=F32)

        pieces += [scores, values]
    for lt in range(tm // V7X_LANES):
        _peer_weight_tile(lt, range(eb // N_KEYS), ht_r, pt_w, t0_ref, t1_ref, th_ref)
        pieces[lt]()


def _peer_dense_kernel(xnt_ref, u_ref, vt_ref, t0a_ref, t1a_ref, tha_ref, t0b_ref, t1b_ref, thb_ref,
                       x1_ref, gf_ref, out_ref, acc_ref, ht0_ref, ht1_ref, pt0_ref, pt1_ref,
                       *, eb, ne, final_norm):
    g = pl.program_id(0)
    tm = acc_ref.shape[1]
    fv = 2 * g - 2

    @pl.when(g == 0)
    def _():
        ht1_ref[...] = jnp.zeros_like(ht1_ref)
        pt0_ref[...] = jnp.zeros_like(pt0_ref)

    @pl.when((g == 0) | (lax.rem(fv, ne) == 0))
    def _():
        acc_ref[...] = jnp.zeros_like(acc_ref)

    _peer_half_step(xnt_ref, u_ref, slice(0, eb), vt_ref, slice(0, eb), acc_ref,
                    ht0_ref, pt0_ref, ht1_ref, pt1_ref, t0a_ref, t1a_ref, tha_ref, eb=eb, tm=tm)
    _peer_half_step(xnt_ref, u_ref, slice(eb, 2 * eb), vt_ref, slice(eb, 2 * eb), acc_ref,
                    ht1_ref, pt1_ref, ht0_ref, pt0_ref, t0b_ref, t1b_ref, thb_ref, eb=eb, tm=tm)

    @pl.when((g > 0) & (lax.rem(fv + 1, ne) == ne - 1))
    def _():
        y = x1_ref[...] + acc_ref[...].T
        if final_norm:
            y = _rms(y, gf_ref[...])
        out_ref[...] = y


def _peer_dense(x1, front, w, gf, *, tm, eb, final_norm):
    xnt, t0, t1, th = front
    n_, d_ = x1.shape
    ne = N_EXPERTS // eb
    assert n_ % tm == 0 and N_EXPERTS % eb == 0 and eb % N_KEYS == 0 and ne % 2 == 0
    assert tm % (2 * V7X_LANES) == 0
    nblocks = (n_ // tm) * ne
    last = nblocks - 1
    tok_mm1 = lambda g: jnp.minimum(2 * g, last) // ne
    tok_a = lambda g: jnp.clip(2 * g - 1, 0, last) // ne
    tok_v = lambda g: jnp.maximum(2 * g - 2, 0) // ne
    blk_a = lambda g: jnp.clip(2 * g - 1, 0, last) % ne
    blk_b = lambda g: jnp.minimum(2 * g, last) % ne
    pair_mm1 = lambda g: (jnp.minimum(2 * g, last) % ne) // 2
    pair_v = lambda g: (jnp.maximum(2 * g - 2, 0) % ne) // 2
    feat = lambda rows, tok: pl.BlockSpec((rows, tm), lambda g: (0, tok(g)))
    nk = PEER_HEADS * N_KEYS
    t0_rows = (eb // N_KEYS) * PEER_HEADS
    return pl.pallas_call(
        functools.partial(_peer_dense_kernel, eb=eb, ne=ne, final_norm=final_norm),
        grid=(nblocks // 2 + 1,),
        in_specs=[
            feat(d_, tok_mm1),
            pl.BlockSpec((2 * eb, d_), lambda g: (pair_mm1(g), 0)),
            pl.BlockSpec((d_, 2 * eb), lambda g: (0, pair_v(g))),
            pl.BlockSpec((t0_rows, tm), lambda g: (blk_a(g), tok_a(g))), feat(nk, tok_a), feat(PEER_HEADS, tok_a),
            pl.BlockSpec((t0_rows, tm), lambda g: (blk_b(g), tok_mm1(g))), feat(nk, tok_mm1),
            feat(PEER_HEADS, tok_mm1),
            pl.BlockSpec((tm, d_), lambda g: (tok_v(g), 0)),
            pl.BlockSpec((1, d_), lambda g: (0, 0)),
        ],
        out_specs=pl.BlockSpec((tm, d_), lambda g: (tok_v(g), 0)),
        out_shape=jax.ShapeDtypeStruct((n_, d_), F32),
        scratch_shapes=[
            pltpu.VMEM((d_, tm), F32),
            pltpu.VMEM((eb, tm), F32), pltpu.VMEM((eb, tm), F32),
            pltpu.VMEM((eb, tm), BF16), pltpu.VMEM((eb, tm), BF16),
        ],
        compiler_params=_cparams("arbitrary"),
        name="peer_dense",
    )(xnt, w["u"], w["vt"], t0, t1, th, t0, t1, th, x1, gf)


def _rope(x, cos, sin_signed):
    half = HEAD_DIM // 2
    lane = lax.broadcasted_iota(jnp.int32, (1, V7X_LANES), 1)
    first = (lane % HEAD_DIM) < half
    outs = []
    for t in range(x.shape[1] // V7X_LANES):
        xt = x[:, t * V7X_LANES:(t + 1) * V7X_LANES]
        swapped = jnp.where(first, pltpu.roll(xt, V7X_LANES - half, axis=1), pltpu.roll(xt, half, axis=1))
        outs.append(xt * cos + swapped * sin_signed)
    return jnp.concatenate(outs, axis=1)


def _qkv_kernel(x_ref, gkv_ref, wkv_ref, gq_ref, wq_ref, cos_ref, sin_ref, q_ref, k_ref, v_ref):
    x = x_ref[...]
    cos = cos_ref[...]
    sin = sin_ref[...]
    kv = jnp.dot(_rms(x, gkv_ref[...]).astype(BF16), wkv_ref[...], preferred_element_type=F32)
    k_ref[...] = _rope(kv[:, :KV_DIM], cos, sin)
    v_ref[...] = kv[:, KV_DIM:]
    q = jnp.dot(_rms(x, gq_ref[...]).astype(BF16), wq_ref[...], preferred_element_type=F32)
    q_ref[...] = (_rope(q, cos, sin) * (HEAD_DIM ** -0.5)).astype(q_ref.dtype)


def _qkv(x, w, cos, sin, *, tm, table_blocks, q_dtype):
    n_, d_ = x.shape
    assert n_ % tm == 0
    const2 = lambda i: (0, 0)
    tok = lambda cols: pl.BlockSpec((tm, cols), lambda i: (i, 0))
    tab = pl.BlockSpec((tm, V7X_LANES), lambda i: (i % table_blocks, 0))
    return pl.pallas_call(
        _qkv_kernel,
        grid=(n_ // tm,),
        in_specs=[tok(d_), pl.BlockSpec((1, d_), const2), pl.BlockSpec((d_, 2 * KV_DIM), const2),
                  pl.BlockSpec((1, d_), const2), pl.BlockSpec((d_, Q_DIM), const2), tab, tab],
        out_specs=[tok(Q_DIM), tok(KV_DIM), tok(KV_DIM)],
        out_shape=[jax.ShapeDtypeStruct((n_, Q_DIM), q_dtype),
                   jax.ShapeDtypeStruct((n_, KV_DIM), F32),
                   jax.ShapeDtypeStruct((n_, KV_DIM), F32)],
        compiler_params=_cparams("arbitrary"),
        name="qkv_rope",
    )(x, w["g_kv"], w["w_kv"], w["g_q"], w["w_q"], cos, sin)


def _sink_attend(s, mask, sink, v):
    s = jnp.where(mask, s, -jnp.inf)
    m = jnp.maximum(jnp.max(s, axis=-1, keepdims=True), sink)
    p = jnp.exp(s - m)
    denom = jnp.sum(p, axis=-1, keepdims=True) + jnp.exp(sink - m)
    return jnp.dot(p.astype(BF16), v, preferred_element_type=F32) / denom


def _swa_prompt_kernel(sink_ref, q_ref, kp_ref, kc_ref, vp_ref, vc_ref, o_ref):
    n = pl.program_id(1)
    c = WINDOW
    k2 = jnp.concatenate([kp_ref[...], kc_ref[...]], axis=0).astype(BF16)
    v2 = jnp.concatenate([vp_ref[...], vc_ref[...]], axis=0).astype(BF16)
    qi = lax.broadcasted_iota(jnp.int32, (c, 2 * c), 0)
    kj = lax.broadcasted_iota(jnp.int32, (c, 2 * c), 1)
    mask = (kj > qi) & (kj <= qi + c) & ((kj >= c) | (n > 0))
    outs = []
    for hd in range(N_HEADS):
        kvh = hd // GROUP
        q = q_ref[:, hd * HEAD_DIM:(hd + 1) * HEAD_DIM]
        kk = k2[:, kvh * HEAD_DIM:(kvh + 1) * HEAD_DIM]
        s = lax.dot_general(q, kk, (((1,), (1,)), ((), ())), preferred_element_type=F32)
        outs.append(_sink_attend(s, mask, sink_ref[hd], v2[:, kvh * HEAD_DIM:(kvh + 1) * HEAD_DIM]))
    o_ref[...] = jnp.concatenate(outs, axis=1).astype(BF16)


def _swa_prompt(q, k, v, sinks, *, batch, seq):
    nb = seq // WINDOW
    cur = lambda cols: pl.BlockSpec((WINDOW, cols), lambda b, n: (b * nb + n, 0))
    prev = lambda cols: pl.BlockSpec((WINDOW, cols), lambda b, n: (b * nb + jnp.maximum(n - 1, 0), 0))
    return pl.pallas_call(
        _swa_prompt_kernel,
        grid=(batch, nb),
        in_specs=[pl.BlockSpec(memory_space=pltpu.SMEM),
                  cur(Q_DIM), prev(KV_DIM), cur(KV_DIM), prev(KV_DIM), cur(KV_DIM)],
        out_specs=cur(Q_DIM),
        out_shape=jax.ShapeDtypeStruct((batch * seq, Q_DIM), BF16),
        compiler_params=_cparams("arbitrary", "arbitrary"),
        name="swa_prompt",
    )(sinks, q, k, k, v, v)


def _swa_sample_kernel(sink_ref, q_ref, kc_ref, kn_ref, vc_ref, vn_ref, o_ref, k2_ref, v2_ref, *, nq, nb):
    c = WINDOW
    rows = GROUP * nq
    qi = lax.broadcasted_iota(jnp.int32, (rows, 2 * c), 0) % nq
    kj = lax.broadcasted_iota(jnp.int32, (rows, 2 * c), 1)
    mask = ((kj < c) & (kj > qi)) | ((kj >= c) & (kj - c <= qi))
    sinks = [jnp.concatenate([jnp.full((nq, 1), sink_ref[hd], F32)
                              for hd in range(kvh * GROUP, (kvh + 1) * GROUP)], axis=0)
             for kvh in range(N_KV_HEADS)]
    for b in range(nb):
        for src_c, src_n, dst in ((kc_ref, kn_ref, k2_ref), (vc_ref, vn_ref, v2_ref)):
            dst[b, 0:c, :] = src_c[b]
            dst[b, c:2 * c, :] = jnp.zeros((c, KV_DIM), F32)
            dst[b, c:c + nq, :] = src_n[b]
        k2 = k2_ref[b].astype(BF16)
        v2 = v2_ref[b].astype(BF16)
        q = q_ref[b]
        outs = [None] * N_HEADS
        for kvh in range(N_KV_HEADS):
            heads = range(kvh * GROUP, (kvh + 1) * GROUP)
            qs = jnp.concatenate([q[:, hd * HEAD_DIM:(hd + 1) * HEAD_DIM] for hd in heads], axis=0).astype(BF16)
            s = lax.dot_general(qs, k2[:, kvh * HEAD_DIM:(kvh + 1) * HEAD_DIM], (((1,), (1,)), ((), ())),
                                preferred_element_type=F32)
            o = _sink_attend(s, mask, sinks[kvh], v2[:, kvh * HEAD_DIM:(kvh + 1) * HEAD_DIM])
            for g, hd in enumerate(heads):
                outs[hd] = o[g * nq:(g + 1) * nq, :]
        o_ref[b] = jnp.concatenate(outs, axis=1)


def _swa_sample(q, kc, kn, vc, vn, sinks):
    b_, nq, _ = q.shape
    assert nq <= WINDOW
    nb = _pick(b_, (8, 4, 2, 1))
    blk = lambda rows, cols: pl.BlockSpec((nb, rows, cols), lambda b: (b, 0, 0))
    return pl.pallas_call(
        functools.partial(_swa_sample_kernel, nq=nq, nb=nb),
        grid=(b_ // nb,),
        in_specs=[pl.BlockSpec(memory_space=pltpu.SMEM),
                  blk(nq, Q_DIM), blk(WINDOW, KV_DIM), blk(nq, KV_DIM), blk(WINDOW, KV_DIM), blk(nq, KV_DIM)],
        out_specs=blk(nq, Q_DIM),
        out_shape=jax.ShapeDtypeStruct((b_, nq, Q_DIM), F32),
        scratch_shapes=[pltpu.VMEM((nb, 2 * WINDOW, KV_DIM), F32), pltpu.VMEM((nb, 2 * WINDOW, KV_DIM), F32)],
        compiler_params=_cparams("arbitrary"),
        name="swa_sample",
    )(sinks, q, kc, kn, vc, vn)


def _rope_tables(pos):
    half = HEAD_DIM // 2
    inv = ROPE_THETA ** (-jnp.arange(half, dtype=F32) / half)
    ang = pos.astype(F32)[:, None] * inv[None, :]
    cos = jnp.cos(ang)
    sin = jnp.sin(ang)
    reps = V7X_LANES // HEAD_DIM
    cos_t = jnp.tile(jnp.concatenate([cos, cos], axis=1), (1, reps))
    sin_t = jnp.tile(jnp.concatenate([-sin, sin], axis=1), (1, reps))
    return cos_t, sin_t


def _pick(n, cands):
    for c in cands:
        if n % c == 0:
            return c
    raise ValueError(f"no tile for {n}")


def _peer(x1, pw, gf, attn=None, *, final_norm):
    n_ = x1.shape[0]
    x1, front = _peer_front(x1, pw, attn, tm=_pick(n_, (256, 128)))
    return _peer_dense(x1, front, pw, gf, tm=_pick(n_, (512, 256, 128)), eb=1024, final_norm=final_norm)


def kernel(x_prompt, x_sample, state_conv, state_h, cache_k, cache_v, rg_norm, rg_w_in, rg_conv_w,
           rg_conv_b, rg_wa, rg_ba, rg_wx, rg_bx, rg_lambda, rg_w_out, kv_norm, w_kv, attn_norm, w_q,
           sinks, w_o, ffn_norm, peer_wq, peer_subkeys, peer_u, peer_v, final_norm):
    d = D_MODEL
    row = lambda a: a.reshape(1, -1).astype(F32)
    rgw = dict(g=row(rg_norm[0]), w_in=rg_w_in[0].astype(BF16), conv_w=rg_conv_w[0],
               conv_b=row(rg_conv_b[0]), wa=rg_wa[0].astype(BF16), ba=row(rg_ba[0]),
               wx=rg_wx[0].astype(BF16), bx=row(rg_bx[0]), lam=row(rg_lambda[0]),
               w_out=rg_w_out[0].astype(BF16))
    peer_w = []
    for l in range(2):
        wq = peer_wq[l].reshape(d, PEER_HEADS, 2, D_KEY // 2).transpose(2, 1, 3, 0).reshape(PQ_DIM, d)
        skx0 = jnp.einsum("nc,hg->nhgc", peer_subkeys[l, 0], jnp.eye(PEER_HEADS, dtype=F32))
        skx0 = skx0.reshape(PEER_HEADS * N_KEYS, PQ_DIM // 2)
        peer_w.append(dict(g=row(ffn_norm[l]), wqt=wq.astype(BF16), skx0=skx0.astype(BF16),
                           sk1=peer_subkeys[l, 1].astype(BF16),
                           u=peer_u[l].astype(BF16), vt=peer_v[l].astype(BF16).T))
    attw = dict(g_kv=row(kv_norm), w_kv=w_kv.astype(BF16), g_q=row(attn_norm[0]), w_q=w_q[0].astype(BF16))
    wo = w_o[0].astype(BF16)
    sink = sinks[0].astype(F32)
    gf = row(final_norm)

    bp, tp, _ = x_prompt.shape
    steps_p = _pick(tp, (256, 128, 64, 32, 16, 8))
    pad_p = V7X_SUBLANES
    x1, conv_p, h_p = _rg_layer(x_prompt, jnp.zeros((bp, pad_p, d), F32), jnp.zeros((bp, 1, d), F32),
                                rgw, stride=1, steps=steps_p)
    xp = _peer(x1.reshape(bp * tp, d), peer_w[0], gf, final_norm=False)
    cos_p, sin_p = _rope_tables(jnp.arange(tp))
    tm_p = _pick(tp, (512, 256, 128))
    q, k, v = _qkv(xp, attw, cos_p, sin_p, tm=tm_p, table_blocks=tp // tm_p, q_dtype=BF16)
    o = _swa_prompt(q, k, v, sink, batch=bp, seq=tp)
    y_prompt = _peer(xp, peer_w[1], gf, attn=(o, wo), final_norm=True).reshape(bp, tp, d)
    k_p = k.reshape(bp, tp, N_KV_HEADS, HEAD_DIM)[:, -WINDOW:]
    v_p = v.reshape(bp, tp, N_KV_HEADS, HEAD_DIM)[:, -WINDOW:]
    conv_p = conv_p[None]
    h_p = h_p.reshape(1, bp, d)

    bs, ts, _ = x_sample.shape
    xs = x_sample.transpose(1, 0, 2).reshape(1, ts * bs, d)
    conv0 = state_conv[0].transpose(1, 0, 2).reshape(1, (CONV_W - 1) * bs, d)
    x1s, conv_s, h_s = _rg_layer(xs, conv0, state_h[0][None], rgw, stride=bs, steps=ts)
    xs1 = _peer(x1s.reshape(ts * bs, d), peer_w[0], gf, final_norm=False)
    xs1 = xs1.reshape(ts, bs, d).transpose(1, 0, 2).reshape(bs * ts, d)
    cos_s, sin_s = _rope_tables(PAST_LEN + jnp.arange(ts))
    cos_s = jnp.tile(cos_s, (bs, 1))
    sin_s = jnp.tile(sin_s, (bs, 1))
    tm_s = _pick(bs * ts, (512, 256, 128, 64, 32, 16, 8))
    qs, ks, vs = _qkv(xs1, attw, cos_s, sin_s, tm=tm_s, table_blocks=(bs * ts) // tm_s, q_dtype=F32)
    ck = cache_k.reshape(bs, WINDOW, KV_DIM)
    cv = cache_v.reshape(bs, WINDOW, KV_DIM)
    kn = ks.reshape(bs, ts, KV_DIM)
    vn = vs.reshape(bs, ts, KV_DIM)
    os_ = _swa_sample(qs.reshape(bs, ts, Q_DIM), ck, kn, cv, vn, sink)
    y_sample = _peer(xs1, peer_w[1], gf, attn=(os_.reshape(bs * ts, Q_DIM), wo),
                     final_norm=True).reshape(bs, ts, d)
    k_s = jnp.concatenate([ck, kn], axis=1)[:, -WINDOW:].reshape(bs, WINDOW, N_KV_HEADS, HEAD_DIM)
    v_s = jnp.concatenate([cv, vn], axis=1)[:, -WINDOW:].reshape(bs, WINDOW, N_KV_HEADS, HEAD_DIM)
    conv_s = conv_s.reshape(CONV_W - 1, bs, d).transpose(1, 0, 2)[None]
    h_s = h_s.reshape(1, bs, d)

    return (y_prompt, y_sample, conv_p, h_p, k_p, v_p, conv_s, h_s, k_s, v_s)
```

```python
import functools
import math

import jax
import jax.numpy as jnp
from jax import lax
from jax.experimental import pallas as pl
from jax.experimental.pallas import tpu as pltpu

D_MODEL = 1024
PAST_LEN = 16384
D_RNN = D_MODEL
RG_BLOCKS = 8
RG_BLOCK_W = D_RNN // RG_BLOCKS
CONV_W = 4
RG_C = 8.0
HEAD_DIM = 64
N_HEADS = D_MODEL // HEAD_DIM
N_KV_HEADS = 4
GROUP = N_HEADS // N_KV_HEADS
WINDOW = 128
ROPE_THETA = 10000.0
N_KEYS = 128
N_EXPERTS = N_KEYS * N_KEYS
PEER_HEADS = 8
PEER_TOPK = 16
D_KEY = 256
EPS = 1e-6

KV_DIM = N_KV_HEADS * HEAD_DIM
Q_DIM = N_HEADS * HEAD_DIM
PQ_DIM = PEER_HEADS * D_KEY

V7X_LANES = 128
V7X_SUBLANES = 8
V7X_VMEM_LIMIT_BYTES = 56 * 1024 * 1024

BF16 = jnp.bfloat16
F32 = jnp.float32


def _cparams(*sem, flags=None):
    return pltpu.CompilerParams(dimension_semantics=sem, vmem_limit_bytes=V7X_VMEM_LIMIT_BYTES, flags=flags)


def _rms(x, g):
    return x * lax.rsqrt(jnp.mean(x * x, axis=-1, keepdims=True) + EPS) * g


def _gelu_tanh(x):
    u2 = (-2.0 * math.sqrt(2.0 / math.pi)) * (x * (1.0 + 0.044715 * (x * x)))
    return x / (1.0 + jnp.exp(u2))


def _sigmoid(x):
    return 1.0 / (1.0 + jnp.exp(-x))


def _rg_kernel(x_ref, conv0_ref, h0_ref, g_ref, win_ref, cw_ref, cb_ref, wa_ref, ba_ref,
               wx_ref, bx_ref, lam_ref, wout_ref,
               y_ref, conv_ref, hlast_ref,
               xbuf_ref, hcar_ref, *, stride, steps, pad):
    rows = stride * steps
    tail = (CONV_W - 1) * stride
    c = pl.program_id(1)

    @pl.when(c == 0)
    def _():
        xbuf_ref[0:pad, :] = conv0_ref[0]
        hcar_ref[...] = h0_ref[0]

    x = x_ref[0]
    xn = _rms(x, g_ref[...]).astype(BF16)
    proj = jnp.dot(xn, win_ref[...], preferred_element_type=F32)
    gate = _gelu_tanh(proj[:, :D_RNN])
    xr = proj[:, D_RNN:]
    xbuf_ref[pad:pad + rows, :] = xr

    y = cb_ref[...] + cw_ref[0:1, :] * xbuf_ref[pad - 3 * stride:pad - 3 * stride + rows, :]
    y = y + cw_ref[1:2, :] * xbuf_ref[pad - 2 * stride:pad - 2 * stride + rows, :]
    y = y + cw_ref[2:3, :] * xbuf_ref[pad - stride:pad - stride + rows, :]
    xc = y + cw_ref[3:4, :] * xr

    conv_ref[0] = xbuf_ref[pad + rows - tail:pad + rows, :]
    xbuf_ref[0:pad, :] = xbuf_ref[rows:rows + pad, :]

    xcb = xc.astype(BF16)
    rs, is_ = [], []
    for n in range(RG_BLOCKS):
        blk = xcb[:, n * RG_BLOCK_W:(n + 1) * RG_BLOCK_W]
        rs.append(jnp.dot(blk, wa_ref[n], preferred_element_type=F32))
        is_.append(jnp.dot(blk, wx_ref[n], preferred_element_type=F32))
    r = _sigmoid(jnp.concatenate(rs, axis=1) + ba_ref[...])
    i = _sigmoid(jnp.concatenate(is_, axis=1) + bx_ref[...])

    nlam = -lam_ref[...]
    softplus = jnp.maximum(nlam, 0.0) + jnp.log1p(jnp.exp(-jnp.abs(nlam)))
    log_a = -RG_C * r * softplus
    a = jnp.exp(log_a)
    mult = jnp.sqrt(-jnp.tanh(log_a) * (a * a + 1.0))
    b = mult * i * xc

    group = V7X_SUBLANES if (stride == 1 and steps % V7X_SUBLANES == 0) else steps
    row = lax.broadcasted_iota(jnp.int32, (rows, D_RNN), 0) % (group * stride)
    s = 1
    while s < group:
        sh = s * stride
        a_sh = pltpu.roll(a, sh, axis=0)
        b_sh = pltpu.roll(b, sh, axis=0)
        m = row >= sh
        b = jnp.where(m, a * b_sh + b, b)
        a = jnp.where(m, a * a_sh, a)
        s *= 2
    grows = group * stride
    carry = hcar_ref[...]
    hs = []
    for g0 in range(0, rows, grows):
        cg = carry if stride == 1 else jnp.concatenate([carry] * group, axis=0)
        hg_ = a[g0:g0 + grows, :] * cg + b[g0:g0 + grows, :]
        hs.append(hg_)
        carry = hg_[grows - stride:grows, :]
    h = hs[0] if len(hs) == 1 else jnp.concatenate(hs, axis=0)
    hl = carry
    hcar_ref[...] = hl
    hlast_ref[0] = hl

    hg = (h * gate).astype(BF16)
    y_ref[0] = x + jnp.dot(hg, wout_ref[...], preferred_element_type=F32)


def _rg_layer(x3, conv0, h0, w, *, stride, steps):
    g_, r_, d_ = x3.shape
    rows = stride * steps
    assert r_ % rows == 0
    pad = conv0.shape[1]
    tail = (CONV_W - 1) * stride
    nchunks = r_ // rows
    const2 = lambda g, c: (0, 0)
    const3 = lambda g, c: (0, 0, 0)
    kern = functools.partial(_rg_kernel, stride=stride, steps=steps, pad=pad)
    return pl.pallas_call(
        kern,
        grid=(g_, nchunks),
        in_specs=[
            pl.BlockSpec((1, rows, d_), lambda g, c: (g, c, 0)),
            pl.BlockSpec((1, pad, d_), lambda g, c: (g, 0, 0)),
            pl.BlockSpec((1, stride, d_), lambda g, c: (g, 0, 0)),
            pl.BlockSpec((1, d_), const2),
            pl.BlockSpec((d_, 2 * D_RNN), const2),
            pl.BlockSpec((CONV_W, D_RNN), const2),
            pl.BlockSpec((1, D_RNN), const2),
            pl.BlockSpec((RG_BLOCKS, RG_BLOCK_W, RG_BLOCK_W), const3),
            pl.BlockSpec((1, D_RNN), const2),
            pl.BlockSpec((RG_BLOCKS, RG_BLOCK_W, RG_BLOCK_W), const3),
            pl.BlockSpec((1, D_RNN), const2),
            pl.BlockSpec((1, D_RNN), const2),
            pl.BlockSpec((D_RNN, d_), const2),
        ],
        out_specs=[
            pl.BlockSpec((1, rows, d_), lambda g, c: (g, c, 0)),
            pl.BlockSpec((1, tail, D_RNN), lambda g, c: (g, 0, 0)),
            pl.BlockSpec((1, stride, D_RNN), lambda g, c: (g, 0, 0)),
        ],
        out_shape=[
            jax.ShapeDtypeStruct((g_, r_, d_), F32),
            jax.ShapeDtypeStruct((g_, tail, D_RNN), F32),
            jax.ShapeDtypeStruct((g_, stride, D_RNN), F32),
        ],
        scratch_shapes=[
            pltpu.VMEM((pad + rows, D_RNN), F32),
            pltpu.VMEM((stride, D_RNN), F32),
        ],
        compiler_params=_cparams("arbitrary", "arbitrary"),
        name="rg_layer",
    )(x3, conv0, h0, w["g"], w["w_in"], w["conv_w"], w["conv_b"], w["wa"], w["ba"],
      w["wx"], w["bx"], w["lam"], w["w_out"])


def _ce(v, i, j):
    hi = jnp.maximum(v[i], v[j])
    lo = jnp.minimum(v[i], v[j])
    v[i], v[j] = hi, lo


def _sort16(v):
    n = 16
    k = 2
    while k <= n:
        j = k // 2
        while j >= 1:
            for i in range(n):
                l = i ^ j
                if l > i:
                    if (i & k) == 0:
                        _ce(v, i, l)
                    else:
                        _ce(v, l, i)
            j //= 2
        k *= 2


def _merge_top16(a, b):
    v = [jnp.maximum(a[k], b[15 - k]) for k in range(16)]
    j = 8
    while j >= 1:
        for i in range(16):
            l = i ^ j
            if l > i:
                _ce(v, i, l)
        j //= 2
    return v


def _top16(vs):
    groups = []
    for g in range(len(vs) // 16):
        lst = list(vs[16 * g:16 * g + 16])
        _sort16(lst)
        groups.append(lst)
    while len(groups) > 1:
        groups = [_merge_top16(groups[i], groups[i + 1]) for i in range(0, len(groups), 2)]
    return groups[0]


def _kth_product(sv0, sv1, mul):
    cands = [mul(sv0[0], sv1[j]) for j in range(PEER_TOPK)]
    for i in range(1, PEER_TOPK):
        for j in range(PEER_TOPK // (i + 1)):
            cands.append(mul(sv0[i], sv1[j]))
    padded = list(cands)
    while len(padded) % 16:
        padded.append(jnp.full_like(cands[0], -1.0))
    return _top16(padded)[PEER_TOPK - 1], cands


def _peer_select(sc_ref, lt):
    sv0 = _top16([sc_ref[0, lt, n * PEER_HEADS:(n + 1) * PEER_HEADS, :] for n in range(N_KEYS)])
    sv1 = _top16([sc_ref[1, lt, pl.ds(n, PEER_HEADS, stride=N_KEYS), :] for n in range(N_KEYS)])
    theta, cands = _kth_product(sv0, sv1, lambda a, b: a * b)
    z = jnp.zeros_like(theta)
    for cnd in cands:
        z = z + jnp.where(cnd >= theta, cnd, 0.0)
    scale = 0.5 / z
    sv0n = [(v * scale).astype(BF16) for v in sv0]
    sv1b = [v.astype(BF16) for v in sv1]
    theta_n, _ = _kth_product(sv0n, sv1b, lambda a, b: (a * b).astype(F32))
    return scale, theta_n


def _peer_front_kernel(*refs, tm, with_attn):
    if with_attn:
        (x_ref, o_ref, wo_ref, g_ref, wqt_ref, skx_ref, sk_ref,
         x1_ref, xnt_ref, t0_ref, t1_ref, th_ref, sc_ref) = refs
        x1 = x_ref[...] + jnp.dot(o_ref[...].astype(BF16), wo_ref[...], preferred_element_type=F32)
        x1_ref[...] = x1
    else:
        (x_ref, g_ref, wqt_ref, skx_ref, sk_ref, xnt_ref, t0_ref, t1_ref, th_ref, sc_ref) = refs
        x1 = x_ref[...]
    nh = PEER_HEADS
    xn = _rms(x1, g_ref[...])
    xnt = xn.T.astype(BF16)
    xnt_ref[...] = xnt
    qt = jnp.dot(wqt_ref[...], xnt, preferred_element_type=F32).astype(BF16)
    s0 = jnp.dot(skx_ref[...], qt[0:PQ_DIM // 2, :], preferred_element_type=F32)
    mx = s0[0:nh, :]
    for n in range(1, N_KEYS):
        mx = jnp.maximum(mx, s0[n * nh:(n + 1) * nh, :])
    for n in range(N_KEYS):
        t = jnp.exp(s0[n * nh:(n + 1) * nh, :] - mx)
        for lt in range(tm // V7X_LANES):
            sc_ref[0, lt, n * nh:(n + 1) * nh, :] = t[:, lt * V7X_LANES:(lt + 1) * V7X_LANES]
    for h in range(nh):
        base = PQ_DIM // 2 + h * (D_KEY // 2)
        s = jnp.dot(sk_ref[...], qt[base:base + D_KEY // 2, :], preferred_element_type=F32)
        t = jnp.exp(s - jnp.max(s, axis=0, keepdims=True)).astype(BF16)
        t1_ref[h * N_KEYS:(h + 1) * N_KEYS, :] = t
        t = t.astype(F32)
        for lt in range(tm // V7X_LANES):
            sc_ref[1, lt, h * N_KEYS:(h + 1) * N_KEYS, :] = t[:, lt * V7X_LANES:(lt + 1) * V7X_LANES]
    for lt in range(tm // V7X_LANES):
        lanes = pl.ds(lt * V7X_LANES, V7X_LANES)
        scale, theta = _peer_select(sc_ref, lt)
        th_ref[:, lanes] = theta
        for n in range(N_KEYS):
            t0_ref[n * nh:(n + 1) * nh, lanes] = (
                sc_ref[0, lt, n * nh:(n + 1) * nh, :] * scale).astype(BF16).astype(F32)


def _peer_front(x, w, attn=None, *, tm):
    n_, d_ = x.shape
    assert n_ % tm == 0
    with_attn = attn is not None
    const2 = lambda i: (0, 0)
    tok = pl.BlockSpec((tm, d_), lambda i: (i, 0))
    in_specs = [tok]
    args = [x]
    if with_attn:
        in_specs += [pl.BlockSpec((tm, Q_DIM), lambda i: (i, 0)), pl.BlockSpec((Q_DIM, d_), const2)]
        args += list(attn)
    in_specs += [
        pl.BlockSpec((1, d_), const2),
        pl.BlockSpec((PQ_DIM, d_), const2),
        pl.BlockSpec((PEER_HEADS * N_KEYS, PQ_DIM // 2), const2),
        pl.BlockSpec((N_KEYS, D_KEY // 2), const2),
    ]
    args += [w["g"], w["wqt"], w["skx0"], w["sk1"]]
    feat = lambda rows: pl.BlockSpec((rows, tm), lambda i: (0, i))
    out_specs = [feat(d_), feat(PEER_HEADS * N_KEYS), feat(PEER_HEADS * N_KEYS), feat(PEER_HEADS)]
    out_shape = [
        jax.ShapeDtypeStruct((d_, n_), BF16),
        jax.ShapeDtypeStruct((PEER_HEADS * N_KEYS, n_), F32),
        jax.ShapeDtypeStruct((PEER_HEADS * N_KEYS, n_), BF16),
        jax.ShapeDtypeStruct((PEER_HEADS, n_), F32),
    ]
    if with_attn:
        out_specs = [tok] + out_specs
        out_shape = [jax.ShapeDtypeStruct((n_, d_), F32)] + out_shape
    outs = pl.pallas_call(
        functools.partial(_peer_front_kernel, tm=tm, with_attn=with_attn),
        grid=(n_ // tm,),
        in_specs=in_specs,
        out_specs=out_specs,
        out_shape=out_shape,
        scratch_shapes=[pltpu.VMEM((2, tm // V7X_LANES, PEER_HEADS * N_KEYS, V7X_LANES), F32)],
        compiler_params=_cparams("arbitrary"),
        name="peer_front_attn" if with_attn else "peer_front",
    )(*args)
    if with_attn:
        return outs[0], outs[1:]
    return x, outs


def _peer_weight_tile(lt, keys, ht_ref, pt_ref, t0_ref, t1_ref, th_ref):
    rb = 2 * V7X_SUBLANES
    lanes = slice(lt * V7X_LANES, (lt + 1) * V7X_LANES)
    bcast = lambda row: jnp.broadcast_to(row, (rb, V7X_LANES)).astype(BF16)
    th = [bcast(th_ref[h:h + 1, lanes]) for h in range(PEER_HEADS)]
    zero = jnp.zeros((rb, V7X_LANES), BF16)
    for al in keys:
        ta = [bcast(t0_ref[al * PEER_HEADS + h:al * PEER_HEADS + h + 1, lanes]) for h in range(PEER_HEADS)]
        for bg in range(N_KEYS // rb):
            w = None
            for h in range(PEER_HEADS):
                prod = ta[h] * t1_ref[h * N_KEYS + bg * rb:h * N_KEYS + (bg + 1) * rb, lanes]
                wh = jnp.where(prod >= th[h], prod, zero)
                w = wh if w is None else w + wh
            r0 = al * N_KEYS + bg * rb
            hv = ht_ref[r0:r0 + rb, lanes]
            gl = hv * (1.0 + lax.erf(hv * (1.0 / math.sqrt(2.0))))
            pt_ref[r0:r0 + rb, lanes] = gl.astype(BF16) * w


def _peer_half_step(xnt_ref, u_ref, u_rows, vt_ref, vt_cols, acc_ref, ht_w, pt_r, ht_r, pt_w,
                    t0_ref, t1_ref, th_ref, *, eb, tm):
    mxu_w = 2 * V7X_LANES
    pieces = []
    for c in range(tm // mxu_w):
        cols = slice(c * mxu_w, (c + 1) * mxu_w)

        def scores(cols=cols):
            ht_w[:, cols] = jnp.dot(u_ref[u_rows, :], xnt_ref[:, cols], preferred_element_type=F32)

        def values(cols=cols):
            acc_ref[:, cols] += jnp.dot(vt_ref[:, vt_cols], pt_r[:, cols], preferred_element_type=F32)

        pieces += [scores, values]
    for lt in range(tm // V7X_LANES):
        _peer_weight_tile(lt, range(eb // N_KEYS), ht_r, pt_w, t0_ref, t1_ref, th_ref)
        pieces[lt]()


def _peer_dense_kernel(xnt_ref, u_ref, vt_ref, t0a_ref, t1a_ref, tha_ref, t0b_ref, t1b_ref, thb_ref,
                       x1_ref, gf_ref, out_ref, acc_ref, ht0_ref, ht1_ref, pt0_ref, pt1_ref,
                       *, eb, ne, final_norm):
    g = pl.program_id(0)
    tm = acc_ref.shape[1]
    fv = 2 * g - 2

    @pl.when(g == 0)
    def _():
        ht1_ref[...] = jnp.zeros_like(ht1_ref)
        pt0_ref[...] = jnp.zeros_like(pt0_ref)

    @pl.when((g == 0) | (lax.rem(fv, ne) == 0))
    def _():
        acc_ref[...] = jnp.zeros_like(acc_ref)

    _peer_half_step(xnt_ref, u_ref, slice(0, eb), vt_ref, slice(0, eb), acc_ref,
                    ht0_ref, pt0_ref, ht1_ref, pt1_ref, t0a_ref, t1a_ref, tha_ref, eb=eb, tm=tm)
    _peer_half_step(xnt_ref, u_ref, slice(eb, 2 * eb), vt_ref, slice(eb, 2 * eb), acc_ref,
                    ht1_ref, pt1_ref, ht0_ref, pt0_ref, t0b_ref, t1b_ref, thb_ref, eb=eb, tm=tm)

    @pl.when((g > 0) & (lax.rem(fv + 1, ne) == ne - 1))
    def _():
        y = x1_ref[...] + acc_ref[...].T
        if final_norm:
            y = _rms(y, gf_ref[...])
        out_ref[...] = y


def _peer_dense(x1, front, w, gf, *, tm, eb, final_norm):
    xnt, t0, t1, th = front
    n_, d_ = x1.shape
    ne = N_EXPERTS // eb
    assert n_ % tm == 0 and N_EXPERTS % eb == 0 and eb % N_KEYS == 0 and ne % 2 == 0
    assert tm % (2 * V7X_LANES) == 0
    nblocks = (n_ // tm) * ne
    last = nblocks - 1
    tok_mm1 = lambda g: jnp.minimum(2 * g, last) // ne
    tok_a = lambda g: jnp.clip(2 * g - 1, 0, last) // ne
    tok_v = lambda g: jnp.maximum(2 * g - 2, 0) // ne
    blk_a = lambda g: jnp.clip(2 * g - 1, 0, last) % ne
    blk_b = lambda g: jnp.minimum(2 * g, last) % ne
    pair_mm1 = lambda g: (jnp.minimum(2 * g, last) % ne) // 2
    pair_v = lambda g: (jnp.maximum(2 * g - 2, 0) % ne) // 2
    feat = lambda rows, tok: pl.BlockSpec((rows, tm), lambda g: (0, tok(g)))
    nk = PEER_HEADS * N_KEYS
    t0_rows = (eb // N_KEYS) * PEER_HEADS
    return pl.pallas_call(
        functools.partial(_peer_dense_kernel, eb=eb, ne=ne, final_norm=final_norm),
        grid=(nblocks // 2 + 1,),
        in_specs=[
            feat(d_, tok_mm1),
            pl.BlockSpec((2 * eb, d_), lambda g: (pair_mm1(g), 0)),
            pl.BlockSpec((d_, 2 * eb), lambda g: (0, pair_v(g))),
            pl.BlockSpec((t0_rows, tm), lambda g: (blk_a(g), tok_a(g))), feat(nk, tok_a), feat(PEER_HEADS, tok_a),
            pl.BlockSpec((t0_rows, tm), lambda g: (blk_b(g), tok_mm1(g))), feat(nk, tok_mm1),
            feat(PEER_HEADS, tok_mm1),
            pl.BlockSpec((tm, d_), lambda g: (tok_v(g), 0)),
            pl.BlockSpec((1, d_), lambda g: (0, 0)),
        ],
        out_specs=pl.BlockSpec((tm, d_), lambda g: (tok_v(g), 0)),
        out_shape=jax.ShapeDtypeStruct((n_, d_), F32),
        scratch_shapes=[
            pltpu.VMEM((d_, tm), F32),
            pltpu.VMEM((eb, tm), F32), pltpu.VMEM((eb, tm), F32),
            pltpu.VMEM((eb, tm), BF16), pltpu.VMEM((eb, tm), BF16),
        ],
        compiler_params=_cparams("arbitrary"),
        name="peer_dense",
    )(xnt, w["u"], w["vt"], t0, t1, th, t0, t1, th, x1, gf)


def _rope(x, cos, sin_signed):
    half = HEAD_DIM // 2
    lane = lax.broadcasted_iota(jnp.int32, (1, V7X_LANES), 1)
    first = (lane % HEAD_DIM) < half
    outs = []
    for t in range(x.shape[1] // V7X_LANES):
        xt = x[:, t * V7X_LANES:(t + 1) * V7X_LANES]
        swapped = jnp.where(first, pltpu.roll(xt, V7X_LANES - half, axis=1), pltpu.roll(xt, half, axis=1))
        outs.append(xt * cos + swapped * sin_signed)
    return jnp.concatenate(outs, axis=1)


def _qkv_kernel(x_ref, gkv_ref, wkv_ref, gq_ref, wq_ref, cos_ref, sin_ref, q_ref, k_ref, v_ref):
    x = x_ref[...]
    cos = cos_ref[...]
    sin = sin_ref[...]
    kv = jnp.dot(_rms(x, gkv_ref[...]).astype(BF16), wkv_ref[...], preferred_element_type=F32)
    k_ref[...] = _rope(kv[:, :KV_DIM], cos, sin)
    v_ref[...] = kv[:, KV_DIM:]
    q = jnp.dot(_rms(x, gq_ref[...]).astype(BF16), wq_ref[...], preferred_element_type=F32)
    q_ref[...] = (_rope(q, cos, sin) * (HEAD_DIM ** -0.5)).astype(q_ref.dtype)


def _qkv(x, w, cos, sin, *, tm, table_blocks, q_dtype):
    n_, d_ = x.shape
    assert n_ % tm == 0
    const2 = lambda i: (0, 0)
    tok = lambda cols: pl.BlockSpec((tm, cols), lambda i: (i, 0))
    tab = pl.BlockSpec((tm, V7X_LANES), lambda i: (i % table_blocks, 0))
    return pl.pallas_call(
        _qkv_kernel,
        grid=(n_ // tm,),
        in_specs=[tok(d_), pl.BlockSpec((1, d_), const2), pl.BlockSpec((d_, 2 * KV_DIM), const2),
                  pl.BlockSpec((1, d_), const2), pl.BlockSpec((d_, Q_DIM), const2), tab, tab],
        out_specs=[tok(Q_DIM), tok(KV_DIM), tok(KV_DIM)],
        out_shape=[jax.ShapeDtypeStruct((n_, Q_DIM), q_dtype),
                   jax.ShapeDtypeStruct((n_, KV_DIM), F32),
                   jax.ShapeDtypeStruct((n_, KV_DIM), F32)],
        compiler_params=_cparams("arbitrary"),
        name="qkv_rope",
    )(x, w["g_kv"], w["w_kv"], w["g_q"], w["w_q"], cos, sin)


def _sink_attend(s, mask, sink, v):
    s = jnp.where(mask, s, -jnp.inf)
    m = jnp.maximum(jnp.max(s, axis=-1, keepdims=True), sink)
    p = jnp.exp(s - m)
    denom = jnp.sum(p, axis=-1, keepdims=True) + jnp.exp(sink - m)
    return jnp.dot(p.astype(BF16), v, preferred_element_type=F32) / denom


def _swa_prompt_kernel(sink_ref, q_ref, kp_ref, kc_ref, vp_ref, vc_ref, o_ref):
    n = pl.program_id(1)
    c = WINDOW
    k2 = jnp.concatenate([kp_ref[...], kc_ref[...]], axis=0).astype(BF16)
    v2 = jnp.concatenate([vp_ref[...], vc_ref[...]], axis=0).astype(BF16)
    qi = lax.broadcasted_iota(jnp.int32, (c, 2 * c), 0)
    kj = lax.broadcasted_iota(jnp.int32, (c, 2 * c), 1)
    mask = (kj > qi) & (kj <= qi + c) & ((kj >= c) | (n > 0))
    outs = []
    for hd in range(N_HEADS):
        kvh = hd // GROUP
        q = q_ref[:, hd * HEAD_DIM:(hd + 1) * HEAD_DIM]
        kk = k2[:, kvh * HEAD_DIM:(kvh + 1) * HEAD_DIM]
        s = lax.dot_general(q, kk, (((1,), (1,)), ((), ())), preferred_element_type=F32)
        outs.append(_sink_attend(s, mask, sink_ref[hd], v2[:, kvh * HEAD_DIM:(kvh + 1) * HEAD_DIM]))
    o_ref[...] = jnp.concatenate(outs, axis=1).astype(BF16)


def _swa_prompt(q, k, v, sinks, *, batch, seq):
    nb = seq // WINDOW
    cur = lambda cols: pl.BlockSpec((WINDOW, cols), lambda b, n: (b * nb + n, 0))
    prev = lambda cols: pl.BlockSpec((WINDOW, cols), lambda b, n: (b * nb + jnp.maximum(n - 1, 0), 0))
    return pl.pallas_call(
        _swa_prompt_kernel,
        grid=(batch, nb),
        in_specs=[pl.BlockSpec(memory_space=pltpu.SMEM),
                  cur(Q_DIM), prev(KV_DIM), cur(KV_DIM), prev(KV_DIM), cur(KV_DIM)],
        out_specs=cur(Q_DIM),
        out_shape=jax.ShapeDtypeStruct((batch * seq, Q_DIM), BF16),
        compiler_params=_cparams("arbitrary", "arbitrary"),
        name="swa_prompt",
    )(sinks, q, k, k, v, v)


def _swa_sample_kernel(sink_ref, q_ref, kc_ref, kn_ref, vc_ref, vn_ref, o_ref, ko_ref, vo_ref, k2_ref, v2_ref,
                       *, nq, nb):
    c = WINDOW
    rows = GROUP * nq
    qi = lax.broadcasted_iota(jnp.int32, (rows, 2 * c), 0) % nq
    kj = lax.broadcasted_iota(jnp.int32, (rows, 2 * c), 1)
    mask = ((kj < c) & (kj > qi)) | ((kj >= c) & (kj - c <= qi))
    sinks = [jnp.concatenate([jnp.full((nq, 1), sink_ref[hd], F32)
                              for hd in range(kvh * GROUP, (kvh + 1) * GROUP)], axis=0)
             for kvh in range(N_KV_HEADS)]
    for b in range(nb):
        for src_c, src_n, dst in ((kc_ref, kn_ref, k2_ref), (vc_ref, vn_ref, v2_ref)):
            dst[b, 0:c, :] = src_c[b]
            dst[b, c:2 * c, :] = jnp.zeros((c, KV_DIM), F32)
            dst[b, c:c + nq, :] = src_n[b]
        k2 = k2_ref[b].astype(BF16)
        v2 = v2_ref[b].astype(BF16)
        q = q_ref[b]
        outs = [None] * N_HEADS
        for kvh in range(N_KV_HEADS):
            heads = range(kvh * GROUP, (kvh + 1) * GROUP)
            qs = jnp.concatenate([q[:, hd * HEAD_DIM:(hd + 1) * HEAD_DIM] for hd in heads], axis=0).astype(BF16)
            s = lax.dot_general(qs, k2[:, kvh * HEAD_DIM:(kvh + 1) * HEAD_DIM], (((1,), (1,)), ((), ())),
                                preferred_element_type=F32)
            o = _sink_attend(s, mask, sinks[kvh], v2[:, kvh * HEAD_DIM:(kvh + 1) * HEAD_DIM])
            for g, hd in enumerate(heads):
                outs[hd] = o[g * nq:(g + 1) * nq, :]
        o_ref[b] = jnp.concatenate(outs, axis=1)
        for src_c, src_n, dst in ((kc_ref, kn_ref, ko_ref), (vc_ref, vn_ref, vo_ref)):
            dst[b, 0:c - nq, :] = src_c[b, nq:c, :]
            dst[b, c - nq:c, :] = src_n[b]


def _swa_sample(q, kc, kn, vc, vn, sinks):
    b_, nq, _ = q.shape
    assert nq <= WINDOW
    nb = _pick(b_, (8, 4, 2, 1))
    blk = lambda rows, cols: pl.BlockSpec((nb, rows, cols), lambda b: (b, 0, 0))
    return pl.pallas_call(
        functools.partial(_swa_sample_kernel, nq=nq, nb=nb),
        grid=(b_ // nb,),
        in_specs=[pl.BlockSpec(memory_space=pltpu.SMEM),
                  blk(nq, Q_DIM), blk(WINDOW, KV_DIM), blk(nq, KV_DIM), blk(WINDOW, KV_DIM), blk(nq, KV_DIM)],
        out_specs=[blk(nq, Q_DIM), blk(WINDOW, KV_DIM), blk(WINDOW, KV_DIM)],
        out_shape=[jax.ShapeDtypeStruct((b_, nq, Q_DIM), F32),
                   jax.ShapeDtypeStruct((b_, WINDOW, KV_DIM), F32),
                   jax.ShapeDtypeStruct((b_, WINDOW, KV_DIM), F32)],
        scratch_shapes=[pltpu.VMEM((nb, 2 * WINDOW, KV_DIM), F32), pltpu.VMEM((nb, 2 * WINDOW, KV_DIM), F32)],
        compiler_params=_cparams("arbitrary"),
        name="swa_sample",
    )(sinks, q, kc, kn, vc, vn)


def _rope_tables(pos):
    half = HEAD_DIM // 2
    inv = ROPE_THETA ** (-jnp.arange(half, dtype=F32) / half)
    ang = pos.astype(F32)[:, None] * inv[None, :]
    cos = jnp.cos(ang)
    sin = jnp.sin(ang)
    reps = V7X_LANES // HEAD_DIM
    cos_t = jnp.tile(jnp.concatenate([cos, cos], axis=1), (1, reps))
    sin_t = jnp.tile(jnp.concatenate([-sin, sin], axis=1), (1, reps))
    return cos_t, sin_t


def _pick(n, cands):
    for c in cands:
        if n % c == 0:
            return c
    raise ValueError(f"no tile for {n}")


def _peer(x1, pw, gf, attn=None, *, final_norm):
    n_ = x1.shape[0]
    x1, front = _peer_front(x1, pw, attn, tm=_pick(n_, (256, 128)))
    return _peer_dense(x1, front, pw, gf, tm=_pick(n_, (512, 256, 128)), eb=1024, final_norm=final_norm)


def kernel(x_prompt, x_sample, state_conv, state_h, cache_k, cache_v, rg_norm, rg_w_in, rg_conv_w,
           rg_conv_b, rg_wa, rg_ba, rg_wx, rg_bx, rg_lambda, rg_w_out, kv_norm, w_kv, attn_norm, w_q,
           sinks, w_o, ffn_norm, peer_wq, peer_subkeys, peer_u, peer_v, final_norm):
    d = D_MODEL
    row = lambda a: a.reshape(1, -1).astype(F32)
    rgw = dict(g=row(rg_norm[0]), w_in=rg_w_in[0].astype(BF16), conv_w=rg_conv_w[0],
               conv_b=row(rg_conv_b[0]), wa=rg_wa[0].astype(BF16), ba=row(rg_ba[0]),
               wx=rg_wx[0].astype(BF16), bx=row(rg_bx[0]), lam=row(rg_lambda[0]),
               w_out=rg_w_out[0].astype(BF16))
    peer_w = []
    for l in range(2):
        wq = peer_wq[l].reshape(d, PEER_HEADS, 2, D_KEY // 2).transpose(2, 1, 3, 0).reshape(PQ_DIM, d)
        skx0 = jnp.einsum("nc,hg->nhgc", peer_subkeys[l, 0], jnp.eye(PEER_HEADS, dtype=F32))
        skx0 = skx0.reshape(PEER_HEADS * N_KEYS, PQ_DIM // 2)
        peer_w.append(dict(g=row(ffn_norm[l]), wqt=wq.astype(BF16), skx0=skx0.astype(BF16),
                           sk1=peer_subkeys[l, 1].astype(BF16),
                           u=peer_u[l].astype(BF16), vt=peer_v[l].astype(BF16).T))
    attw = dict(g_kv=row(kv_norm), w_kv=w_kv.astype(BF16), g_q=row(attn_norm[0]), w_q=w_q[0].astype(BF16))
    wo = w_o[0].astype(BF16)
    sink = sinks[0].astype(F32)
    gf = row(final_norm)

    bp, tp, _ = x_prompt.shape
    steps_p = _pick(tp, (256, 128, 64, 32, 16, 8))
    pad_p = V7X_SUBLANES
    x1, conv_p, h_p = _rg_layer(x_prompt, jnp.zeros((bp, pad_p, d), F32), jnp.zeros((bp, 1, d), F32),
                                rgw, stride=1, steps=steps_p)
    xp = _peer(x1.reshape(bp * tp, d), peer_w[0], gf, final_norm=False)
    cos_p, sin_p = _rope_tables(jnp.arange(tp))
    tm_p = _pick(tp, (512, 256, 128))
    q, k, v = _qkv(xp, attw, cos_p, sin_p, tm=tm_p, table_blocks=tp // tm_p, q_dtype=BF16)
    o = _swa_prompt(q, k, v, sink, batch=bp, seq=tp)
    y_prompt = _peer(xp, peer_w[1], gf, attn=(o, wo), final_norm=True).reshape(bp, tp, d)
    k_p = k.reshape(bp, tp, N_KV_HEADS, HEAD_DIM)[:, -WINDOW:]
    v_p = v.reshape(bp, tp, N_KV_HEADS, HEAD_DIM)[:, -WINDOW:]
    conv_p = conv_p[None]
    h_p = h_p.reshape(1, bp, d)

    bs, ts, _ = x_sample.shape
    xs = x_sample.transpose(1, 0, 2).reshape(1, ts * bs, d)
    conv0 = state_conv[0].transpose(1, 0, 2).reshape(1, (CONV_W - 1) * bs, d)
    x1s, conv_s, h_s = _rg_layer(xs, conv0, state_h[0][None], rgw, stride=bs, steps=ts)
    xs1 = _peer(x1s.reshape(ts * bs, d), peer_w[0], gf, final_norm=False)
    xs1 = xs1.reshape(ts, bs, d).transpose(1, 0, 2).reshape(bs * ts, d)
    cos_s, sin_s = _rope_tables(PAST_LEN + jnp.arange(ts))
    cos_s = jnp.tile(cos_s, (bs, 1))
    sin_s = jnp.tile(sin_s, (bs, 1))
    tm_s = _pick(bs * ts, (512, 256, 128, 64, 32, 16, 8))
    qs, ks, vs = _qkv(xs1, attw, cos_s, sin_s, tm=tm_s, table_blocks=(bs * ts) // tm_s, q_dtype=F32)
    ck = cache_k.reshape(bs, WINDOW, KV_DIM)
    cv = cache_v.reshape(bs, WINDOW, KV_DIM)
    kn = ks.reshape(bs, ts, KV_DIM)
    vn = vs.reshape(bs, ts, KV_DIM)
    os_, k_s, v_s = _swa_sample(qs.reshape(bs, ts, Q_DIM), ck, kn, cv, vn, sink)
    y_sample = _peer(xs1, peer_w[1], gf, attn=(os_.reshape(bs * ts, Q_DIM), wo),
                     final_norm=True).reshape(bs, ts, d)
    k_s = k_s.reshape(bs, WINDOW, N_KV_HEADS, HEAD_DIM)
    v_s = v_s.reshape(bs, WINDOW, N_KV_HEADS, HEAD_DIM)
    conv_s = conv_s.reshape(CONV_W - 1, bs, d).transpose(1, 0, 2)[None]
    h_s = h_s.reshape(1, bs, d)

    return (y_prompt, y_sample, conv_p, h_p, k_p, v_p, conv_s, h_s, k_s, v_s)
```

```python
import functools
import math

import jax
import jax.numpy as jnp
from jax import lax
from jax.experimental import pallas as pl
from jax.experimental.pallas import tpu as pltpu

D_MODEL = 1024
PAST_LEN = 16384
D_RNN = D_MODEL
RG_BLOCKS = 8
RG_BLOCK_W = D_RNN // RG_BLOCKS
CONV_W = 4
RG_C = 8.0
HEAD_DIM = 64
N_HEADS = D_MODEL // HEAD_DIM
N_KV_HEADS = 4
GROUP = N_HEADS // N_KV_HEADS
WINDOW = 128
ROPE_THETA = 10000.0
N_KEYS = 128
N_EXPERTS = N_KEYS * N_KEYS
PEER_HEADS = 8
PEER_TOPK = 16
D_KEY = 256
EPS = 1e-6

KV_DIM = N_KV_HEADS * HEAD_DIM
Q_DIM = N_HEADS * HEAD_DIM
PQ_DIM = PEER_HEADS * D_KEY

V7X_LANES = 128
V7X_SUBLANES = 8
V7X_VMEM_LIMIT_BYTES = 56 * 1024 * 1024

BF16 = jnp.bfloat16
F32 = jnp.float32


def _cparams(*sem, flags=None):
    return pltpu.CompilerParams(dimension_semantics=sem, vmem_limit_bytes=V7X_VMEM_LIMIT_BYTES, flags=flags)


def _rms(x, g):
    return x * lax.rsqrt(jnp.mean(x * x, axis=-1, keepdims=True) + EPS) * g


def _gelu_tanh(x):
    u2 = (-2.0 * math.sqrt(2.0 / math.pi)) * (x * (1.0 + 0.044715 * (x * x)))
    return x / (1.0 + jnp.exp(u2))


def _sigmoid(x):
    return 1.0 / (1.0 + jnp.exp(-x))


def _rg_kernel(x_ref, conv0_ref, h0_ref, g_ref, win_ref, cw_ref, cb_ref, wa_ref, ba_ref,
               wx_ref, bx_ref, lam_ref, wout_ref,
               y_ref, conv_ref, hlast_ref,
               xbuf_ref, hcar_ref, *, stride, steps, pad):
    rows = stride * steps
    tail = (CONV_W - 1) * stride
    c = pl.program_id(1)

    @pl.when(c == 0)
    def _():
        xbuf_ref[0:pad, :] = conv0_ref[0]
        hcar_ref[...] = h0_ref[0]

    x = x_ref[0]
    xn = _rms(x, g_ref[...]).astype(BF16)
    proj = jnp.dot(xn, win_ref[...], preferred_element_type=F32)
    gate = _gelu_tanh(proj[:, :D_RNN])
    xr = proj[:, D_RNN:]
    xbuf_ref[pad:pad + rows, :] = xr

    y = cb_ref[...] + cw_ref[0:1, :] * xbuf_ref[pad - 3 * stride:pad - 3 * stride + rows, :]
    y = y + cw_ref[1:2, :] * xbuf_ref[pad - 2 * stride:pad - 2 * stride + rows, :]
    y = y + cw_ref[2:3, :] * xbuf_ref[pad - stride:pad - stride + rows, :]
    xc = y + cw_ref[3:4, :] * xr

    conv_ref[0] = xbuf_ref[pad + rows - tail:pad + rows, :]
    xbuf_ref[0:pad, :] = xbuf_ref[rows:rows + pad, :]

    xcb = xc.astype(BF16)
    rs, is_ = [], []
    for n in range(RG_BLOCKS):
        blk = xcb[:, n * RG_BLOCK_W:(n + 1) * RG_BLOCK_W]
        rs.append(jnp.dot(blk, wa_ref[n], preferred_element_type=F32))
        is_.append(jnp.dot(blk, wx_ref[n], preferred_element_type=F32))
    r = _sigmoid(jnp.concatenate(rs, axis=1) + ba_ref[...])
    i = _sigmoid(jnp.concatenate(is_, axis=1) + bx_ref[...])

    nlam = -lam_ref[...]
    softplus = jnp.maximum(nlam, 0.0) + jnp.log1p(jnp.exp(-jnp.abs(nlam)))
    log_a = -RG_C * r * softplus
    a = jnp.exp(log_a)
    mult = jnp.sqrt(-jnp.tanh(log_a) * (a * a + 1.0))
    b = mult * i * xc

    group = V7X_SUBLANES if (stride == 1 and steps % V7X_SUBLANES == 0) else steps
    row = lax.broadcasted_iota(jnp.int32, (rows, D_RNN), 0) % (group * stride)
    s = 1
    while s < group:
        sh = s * stride
        a_sh = pltpu.roll(a, sh, axis=0)
        b_sh = pltpu.roll(b, sh, axis=0)
        m = row >= sh
        b = jnp.where(m, a * b_sh + b, b)
        a = jnp.where(m, a * a_sh, a)
        s *= 2
    grows = group * stride
    carry = hcar_ref[...]
    hs = []
    for g0 in range(0, rows, grows):
        cg = carry if stride == 1 else jnp.concatenate([carry] * group, axis=0)
        hg_ = a[g0:g0 + grows, :] * cg + b[g0:g0 + grows, :]
        hs.append(hg_)
        carry = hg_[grows - stride:grows, :]
    h = hs[0] if len(hs) == 1 else jnp.concatenate(hs, axis=0)
    hl = carry
    hcar_ref[...] = hl
    hlast_ref[0] = hl

    hg = (h * gate).astype(BF16)
    y_ref[0] = x + jnp.dot(hg, wout_ref[...], preferred_element_type=F32)


def _rg_layer(x3, conv0, h0, w, *, stride, steps):
    g_, r_, d_ = x3.shape
    rows = stride * steps
    assert r_ % rows == 0
    pad = conv0.shape[1]
    tail = (CONV_W - 1) * stride
    nchunks = r_ // rows
    const2 = lambda g, c: (0, 0)
    const3 = lambda g, c: (0, 0, 0)
    kern = functools.partial(_rg_kernel, stride=stride, steps=steps, pad=pad)
    return pl.pallas_call(
        kern,
        grid=(g_, nchunks),
        in_specs=[
            pl.BlockSpec((1, rows, d_), lambda g, c: (g, c, 0)),
            pl.BlockSpec((1, pad, d_), lambda g, c: (g, 0, 0)),
            pl.BlockSpec((1, stride, d_), lambda g, c: (g, 0, 0)),
            pl.BlockSpec((1, d_), const2),
            pl.BlockSpec((d_, 2 * D_RNN), const2),
            pl.BlockSpec((CONV_W, D_RNN), const2),
            pl.BlockSpec((1, D_RNN), const2),
            pl.BlockSpec((RG_BLOCKS, RG_BLOCK_W, RG_BLOCK_W), const3),
            pl.BlockSpec((1, D_RNN), const2),
            pl.BlockSpec((RG_BLOCKS, RG_BLOCK_W, RG_BLOCK_W), const3),
            pl.BlockSpec((1, D_RNN), const2),
            pl.BlockSpec((1, D_RNN), const2),
            pl.BlockSpec((D_RNN, d_), const2),
        ],
        out_specs=[
            pl.BlockSpec((1, rows, d_), lambda g, c: (g, c, 0)),
            pl.BlockSpec((1, tail, D_RNN), lambda g, c: (g, 0, 0)),
            pl.BlockSpec((1, stride, D_RNN), lambda g, c: (g, 0, 0)),
        ],
        out_shape=[
            jax.ShapeDtypeStruct((g_, r_, d_), F32),
            jax.ShapeDtypeStruct((g_, tail, D_RNN), F32),
            jax.ShapeDtypeStruct((g_, stride, D_RNN), F32),
        ],
        scratch_shapes=[
            pltpu.VMEM((pad + rows, D_RNN), F32),
            pltpu.VMEM((stride, D_RNN), F32),
        ],
        compiler_params=_cparams("arbitrary", "arbitrary"),
        name="rg_layer",
    )(x3, conv0, h0, w["g"], w["w_in"], w["conv_w"], w["conv_b"], w["wa"], w["ba"],
      w["wx"], w["bx"], w["lam"], w["w_out"])


def _ce(v, i, j):
    hi = jnp.maximum(v[i], v[j])
    lo = jnp.minimum(v[i], v[j])
    v[i], v[j] = hi, lo


def _sort16(v):
    n = 16
    k = 2
    while k <= n:
        j = k // 2
        while j >= 1:
            for i in range(n):
                l = i ^ j
                if l > i:
                    if (i & k) == 0:
                        _ce(v, i, l)
                    else:
                        _ce(v, l, i)
            j //= 2
        k *= 2


def _merge_top16(a, b):
    v = [jnp.maximum(a[k], b[15 - k]) for k in range(16)]
    j = 8
    while j >= 1:
        for i in range(16):
            l = i ^ j
            if l > i:
                _ce(v, i, l)
        j //= 2
    return v


def _top16(vs):
    groups = []
    for g in range(len(vs) // 16):
        lst = list(vs[16 * g:16 * g + 16])
        _sort16(lst)
        groups.append(lst)
    while len(groups) > 1:
        groups = [_merge_top16(groups[i], groups[i + 1]) for i in range(0, len(groups), 2)]
    return groups[0]


def _kth_product(sv0, sv1, mul):
    cands = [mul(sv0[0], sv1[j]) for j in range(PEER_TOPK)]
    for i in range(1, PEER_TOPK):
        for j in range(PEER_TOPK // (i + 1)):
            cands.append(mul(sv0[i], sv1[j]))
    padded = list(cands)
    while len(padded) % 16:
        padded.append(jnp.full_like(cands[0], -1.0))
    return _top16(padded)[PEER_TOPK - 1], cands


def _peer_select(sc_ref, lt):
    sv0 = _top16([sc_ref[0, lt, n * PEER_HEADS:(n + 1) * PEER_HEADS, :] for n in range(N_KEYS)])
    sv1 = _top16([sc_ref[1, lt, pl.ds(n, PEER_HEADS, stride=N_KEYS), :] for n in range(N_KEYS)])
    theta, cands = _kth_product(sv0, sv1, lambda a, b: a * b)
    z = jnp.zeros_like(theta)
    for cnd in cands:
        z = z + jnp.where(cnd >= theta, cnd, 0.0)
    scale = 0.5 / z
    sv0n = [(v * scale).astype(BF16) for v in sv0]
    sv1b = [v.astype(BF16) for v in sv1]
    theta_n, _ = _kth_product(sv0n, sv1b, lambda a, b: (a * b).astype(F32))
    return scale, theta_n


def _peer_front_kernel(*refs, tm, with_attn):
    if with_attn:
        (x_ref, o_ref, wo_ref, g_ref, wqt_ref, skx_ref, sk_ref,
         x1_ref, xnt_ref, t0_ref, t1_ref, th_ref, sc_ref) = refs
        x1 = x_ref[...] + jnp.dot(o_ref[...].astype(BF16), wo_ref[...], preferred_element_type=F32)
        x1_ref[...] = x1
    else:
        (x_ref, g_ref, wqt_ref, skx_ref, sk_ref, xnt_ref, t0_ref, t1_ref, th_ref, sc_ref) = refs
        x1 = x_ref[...]
    nh = PEER_HEADS
    xn = _rms(x1, g_ref[...])
    xnt = xn.T.astype(BF16)
    xnt_ref[...] = xnt
    qt = jnp.dot(wqt_ref[...], xnt, preferred_element_type=F32).astype(BF16)
    s0 = jnp.dot(skx_ref[...], qt[0:PQ_DIM // 2, :], preferred_element_type=F32)
    mx = s0[0:nh, :]
    for n in range(1, N_KEYS):
        mx = jnp.maximum(mx, s0[n * nh:(n + 1) * nh, :])
    for n in range(N_KEYS):
        t = jnp.exp(s0[n * nh:(n + 1) * nh, :] - mx)
        for lt in range(tm // V7X_LANES):
            sc_ref[0, lt, n * nh:(n + 1) * nh, :] = t[:, lt * V7X_LANES:(lt + 1) * V7X_LANES]
    for h in range(nh):
        base = PQ_DIM // 2 + h * (D_KEY // 2)
        s = jnp.dot(sk_ref[...], qt[base:base + D_KEY // 2, :], preferred_element_type=F32)
        t = jnp.exp(s - jnp.max(s, axis=0, keepdims=True)).astype(BF16)
        t1_ref[h * N_KEYS:(h + 1) * N_KEYS, :] = t
        t = t.astype(F32)
        for lt in range(tm // V7X_LANES):
            sc_ref[1, lt, h * N_KEYS:(h + 1) * N_KEYS, :] = t[:, lt * V7X_LANES:(lt + 1) * V7X_LANES]
    for lt in range(tm // V7X_LANES):
        lanes = pl.ds(lt * V7X_LANES, V7X_LANES)
        scale, theta = _peer_select(sc_ref, lt)
        th_ref[:, lanes] = theta
        for n in range(N_KEYS):
            t0_ref[n * nh:(n + 1) * nh, lanes] = (
                sc_ref[0, lt, n * nh:(n + 1) * nh, :] * scale).astype(BF16).astype(F32)


def _peer_front(x, w, attn=None, *, tm):
    n_, d_ = x.shape
    assert n_ % tm == 0
    with_attn = attn is not None
    const2 = lambda i: (0, 0)
    tok = pl.BlockSpec((tm, d_), lambda i: (i, 0))
    in_specs = [tok]
    args = [x]
    if with_attn:
        in_specs += [pl.BlockSpec((tm, Q_DIM), lambda i: (i, 0)), pl.BlockSpec((Q_DIM, d_), const2)]
        args += list(attn)
    in_specs += [
        pl.BlockSpec((1, d_), const2),
        pl.BlockSpec((PQ_DIM, d_), const2),
        pl.BlockSpec((PEER_HEADS * N_KEYS, PQ_DIM // 2), const2),
        pl.BlockSpec((N_KEYS, D_KEY // 2), const2),
    ]
    args += [w["g"], w["wqt"], w["skx0"], w["sk1"]]
    feat = lambda rows: pl.BlockSpec((rows, tm), lambda i: (0, i))
    out_specs = [feat(d_), feat(PEER_HEADS * N_KEYS), feat(PEER_HEADS * N_KEYS), feat(PEER_HEADS)]
    out_shape = [
        jax.ShapeDtypeStruct((d_, n_), BF16),
        jax.ShapeDtypeStruct((PEER_HEADS * N_KEYS, n_), F32),
        jax.ShapeDtypeStruct((PEER_HEADS * N_KEYS, n_), BF16),
        jax.ShapeDtypeStruct((PEER_HEADS, n_), F32),
    ]
    if with_attn:
        out_specs = [tok] + out_specs
        out_shape = [jax.ShapeDtypeStruct((n_, d_), F32)] + out_shape
    outs = pl.pallas_call(
        functools.partial(_peer_front_kernel, tm=tm, with_attn=with_attn),
        grid=(n_ // tm,),
        in_specs=in_specs,
        out_specs=out_specs,
        out_shape=out_shape,
        scratch_shapes=[pltpu.VMEM((2, tm // V7X_LANES, PEER_HEADS * N_KEYS, V7X_LANES), F32)],
        compiler_params=_cparams("arbitrary"),
        name="peer_front_attn" if with_attn else "peer_front",
    )(*args)
    if with_attn:
        return outs[0], outs[1:]
    return x, outs


def _peer_weight_tile(lt, keys, ht_ref, pt_ref, t0_ref, t1_ref, th_ref):
    rb = 2 * V7X_SUBLANES
    lanes = slice(lt * V7X_LANES, (lt + 1) * V7X_LANES)
    bcast = lambda row: jnp.broadcast_to(row, (rb, V7X_LANES)).astype(BF16)
    th = [bcast(th_ref[h:h + 1, lanes]) for h in range(PEER_HEADS)]
    zero = jnp.zeros((rb, V7X_LANES), BF16)
    for al in keys:
        ta = [bcast(t0_ref[al * PEER_HEADS + h:al * PEER_HEADS + h + 1, lanes]) for h in range(PEER_HEADS)]
        for bg in range(N_KEYS // rb):
            w = None
            for h in range(PEER_HEADS):
                prod = ta[h] * t1_ref[h * N_KEYS + bg * rb:h * N_KEYS + (bg + 1) * rb, lanes]
                wh = jnp.where(prod >= th[h], prod, zero)
                w = wh if w is None else w + wh
            r0 = al * N_KEYS + bg * rb
            hv = ht_ref[r0:r0 + rb, lanes]
            gl = hv * (1.0 + lax.erf(hv * (1.0 / math.sqrt(2.0))))
            pt_ref[r0:r0 + rb, lanes] = gl.astype(BF16) * w


def _peer_half_step(xnt_ref, u_ref, u_rows, vt_ref, vt_cols, acc_ref, ht_w, pt_r, ht_r, pt_w,
                    t0_ref, t1_ref, th_ref, *, eb, tm):
    mxu_w = 2 * V7X_LANES
    pieces = []
    for c in range(tm // mxu_w):
        cols = slice(c * mxu_w, (c + 1) * mxu_w)

        def scores(cols=cols):
            ht_w[:, cols] = jnp.dot(u_ref[u_rows, :], xnt_ref[:, cols], preferred_element_type=F32)

        def values(cols=cols):
            acc_ref[:, cols] += jnp.dot(vt_ref[:, vt_cols], pt_r[:, cols], preferred_element_type=F32)

        pieces += [scores, values]
    for lt in range(tm // V7X_LANES):
        _peer_weight_tile(lt, range(eb // N_KEYS), ht_r, pt_w, t0_ref, t1_ref, th_ref)
        pieces[lt]()


def _peer_dense_kernel(xnt_ref, u_ref, vt_ref, t0a_ref, t1a_ref, tha_ref, t0b_ref, t1b_ref, thb_ref,
                       x1_ref, gf_ref, out_ref, acc_ref, ht0_ref, ht1_ref, pt0_ref, pt1_ref,
                       *, eb, ne, final_norm):
    g = pl.program_id(0)
    tm = acc_ref.shape[1]
    fv = 2 * g - 2

    @pl.when(g == 0)
    def _():
        ht1_ref[...] = jnp.zeros_like(ht1_ref)
        pt0_ref[...] = jnp.zeros_like(pt0_ref)

    @pl.when((g == 0) | (lax.rem(fv, ne) == 0))
    def _():
        acc_ref[...] = jnp.zeros_like(acc_ref)

    _peer_half_step(xnt_ref, u_ref, slice(0, eb), vt_ref, slice(0, eb), acc_ref,
                    ht0_ref, pt0_ref, ht1_ref, pt1_ref, t0a_ref, t1a_ref, tha_ref, eb=eb, tm=tm)
    _peer_half_step(xnt_ref, u_ref, slice(eb, 2 * eb), vt_ref, slice(eb, 2 * eb), acc_ref,
                    ht1_ref, pt1_ref, ht0_ref, pt0_ref, t0b_ref, t1b_ref, thb_ref, eb=eb, tm=tm)

    @pl.when((g > 0) & (lax.rem(fv + 1, ne) == ne - 1))
    def _():
        y = x1_ref[...] + acc_ref[...].T
        if final_norm:
            y = _rms(y, gf_ref[...])
        out_ref[...] = y


def _peer_dense(x1, front, w, gf, *, tm, eb, final_norm):
    xnt, t0, t1, th = front
    n_, d_ = x1.shape
    ne = N_EXPERTS // eb
    assert n_ % tm == 0 and N_EXPERTS % eb == 0 and eb % N_KEYS == 0 and ne % 2 == 0
    assert tm % (2 * V7X_LANES) == 0
    nblocks = (n_ // tm) * ne
    last = nblocks - 1
    tok_mm1 = lambda g: jnp.minimum(2 * g, last) // ne
    tok_a = lambda g: jnp.clip(2 * g - 1, 0, last) // ne
    tok_v = lambda g: jnp.maximum(2 * g - 2, 0) // ne
    blk_a = lambda g: jnp.clip(2 * g - 1, 0, last) % ne
    blk_b = lambda g: jnp.minimum(2 * g, last) % ne
    pair_mm1 = lambda g: (jnp.minimum(2 * g, last) % ne) // 2
    pair_v = lambda g: (jnp.maximum(2 * g - 2, 0) % ne) // 2
    feat = lambda rows, tok: pl.BlockSpec((rows, tm), lambda g: (0, tok(g)))
    nk = PEER_HEADS * N_KEYS
    t0_rows = (eb // N_KEYS) * PEER_HEADS
    return pl.pallas_call(
        functools.partial(_peer_dense_kernel, eb=eb, ne=ne, final_norm=final_norm),
        grid=(nblocks // 2 + 1,),
        in_specs=[
            feat(d_, tok_mm1),
            pl.BlockSpec((2 * eb, d_), lambda g: (pair_mm1(g), 0)),
            pl.BlockSpec((d_, 2 * eb), lambda g: (0, pair_v(g))),
            pl.BlockSpec((t0_rows, tm), lambda g: (blk_a(g), tok_a(g))), feat(nk, tok_a), feat(PEER_HEADS, tok_a),
            pl.BlockSpec((t0_rows, tm), lambda g: (blk_b(g), tok_mm1(g))), feat(nk, tok_mm1),
            feat(PEER_HEADS, tok_mm1),
            pl.BlockSpec((tm, d_), lambda g: (tok_v(g), 0)),
            pl.BlockSpec((1, d_), lambda g: (0, 0)),
        ],
        out_specs=pl.BlockSpec((tm, d_), lambda g: (tok_v(g), 0)),
        out_shape=jax.ShapeDtypeStruct((n_, d_), F32),
        scratch_shapes=[
            pltpu.VMEM((d_, tm), F32),
            pltpu.VMEM((eb, tm), F32), pltpu.VMEM((eb, tm), F32),
            pltpu.VMEM((eb, tm), BF16), pltpu.VMEM((eb, tm), BF16),
        ],
        compiler_params=_cparams("arbitrary"),
        name="peer_dense",
    )(xnt, w["u"], w["vt"], t0, t1, th, t0, t1, th, x1, gf)


def _rope(x, cos, sin_signed):
    half = HEAD_DIM // 2
    lane = lax.broadcasted_iota(jnp.int32, (1, V7X_LANES), 1)
    first = (lane % HEAD_DIM) < half
    outs = []
    for t in range(x.shape[1] // V7X_LANES):
        xt = x[:, t * V7X_LANES:(t + 1) * V7X_LANES]
        swapped = jnp.where(first, pltpu.roll(xt, V7X_LANES - half, axis=1), pltpu.roll(xt, half, axis=1))
        outs.append(xt * cos + swapped * sin_signed)
    return jnp.concatenate(outs, axis=1)


def _qkv_kernel(x_ref, gkv_ref, wkv_ref, gq_ref, wq_ref, cos_ref, sin_ref, q_ref, k_ref, v_ref):
    x = x_ref[...]
    cos = cos_ref[...]
    sin = sin_ref[...]
    kv = jnp.dot(_rms(x, gkv_ref[...]).astype(BF16), wkv_ref[...], preferred_element_type=F32)
    k_ref[...] = _rope(kv[:, :KV_DIM], cos, sin)
    v_ref[...] = kv[:, KV_DIM:]
    q = jnp.dot(_rms(x, gq_ref[...]).astype(BF16), wq_ref[...], preferred_element_type=F32)
    q_ref[...] = (_rope(q, cos, sin) * (HEAD_DIM ** -0.5)).astype(q_ref.dtype)


def _qkv(x, w, cos, sin, *, tm, table_blocks, q_dtype):
    n_, d_ = x.shape
    assert n_ % tm == 0
    const2 = lambda i: (0, 0)
    tok = lambda cols: pl.BlockSpec((tm, cols), lambda i: (i, 0))
    tab = pl.BlockSpec((tm, V7X_LANES), lambda i: (i % table_blocks, 0))
    return pl.pallas_call(
        _qkv_kernel,
        grid=(n_ // tm,),
        in_specs=[tok(d_), pl.BlockSpec((1, d_), const2), pl.BlockSpec((d_, 2 * KV_DIM), const2),
                  pl.BlockSpec((1, d_), const2), pl.BlockSpec((d_, Q_DIM), const2), tab, tab],
        out_specs=[tok(Q_DIM), tok(KV_DIM), tok(KV_DIM)],
        out_shape=[jax.ShapeDtypeStruct((n_, Q_DIM), q_dtype),
                   jax.ShapeDtypeStruct((n_, KV_DIM), F32),
                   jax.ShapeDtypeStruct((n_, KV_DIM), F32)],
        compiler_params=_cparams("arbitrary"),
        name="qkv_rope",
    )(x, w["g_kv"], w["w_kv"], w["g_q"], w["w_q"], cos, sin)


def _sink_attend(s, mask, sink, v):
    s = jnp.where(mask, s, -jnp.inf)
    m = jnp.maximum(jnp.max(s, axis=-1, keepdims=True), sink)
    p = jnp.exp(s - m)
    denom = jnp.sum(p, axis=-1, keepdims=True) + jnp.exp(sink - m)
    return jnp.dot(p.astype(BF16), v, preferred_element_type=F32) / denom


def _swa_prompt_kernel(sink_ref, q_ref, kp_ref, kc_ref, vp_ref, vc_ref, o_ref):
    n = pl.program_id(1)
    c = WINDOW
    k2 = jnp.concatenate([kp_ref[...], kc_ref[...]], axis=0).astype(BF16)
    v2 = jnp.concatenate([vp_ref[...], vc_ref[...]], axis=0).astype(BF16)
    qi = lax.broadcasted_iota(jnp.int32, (GROUP * c, 2 * c), 0) % c
    kj = lax.broadcasted_iota(jnp.int32, (GROUP * c, 2 * c), 1)
    mask = (kj > qi) & (kj <= qi + c) & ((kj >= c) | (n > 0))
    outs = [None] * N_HEADS
    for kvh in range(N_KV_HEADS):
        heads = range(kvh * GROUP, (kvh + 1) * GROUP)
        qs = jnp.concatenate([q_ref[:, hd * HEAD_DIM:(hd + 1) * HEAD_DIM] for hd in heads], axis=0)
        sink = jnp.concatenate([jnp.full((c, 1), sink_ref[hd], F32) for hd in heads], axis=0)
        s = lax.dot_general(qs, k2[:, kvh * HEAD_DIM:(kvh + 1) * HEAD_DIM], (((1,), (1,)), ((), ())),
                            preferred_element_type=F32)
        o = _sink_attend(s, mask, sink, v2[:, kvh * HEAD_DIM:(kvh + 1) * HEAD_DIM])
        for g, hd in enumerate(heads):
            outs[hd] = o[g * c:(g + 1) * c, :]
    o_ref[...] = jnp.concatenate(outs, axis=1).astype(BF16)


def _swa_prompt(q, k, v, sinks, *, batch, seq):
    nb = seq // WINDOW
    cur = lambda cols: pl.BlockSpec((WINDOW, cols), lambda b, n: (b * nb + n, 0))
    prev = lambda cols: pl.BlockSpec((WINDOW, cols), lambda b, n: (b * nb + jnp.maximum(n - 1, 0), 0))
    return pl.pallas_call(
        _swa_prompt_kernel,
        grid=(batch, nb),
        in_specs=[pl.BlockSpec(memory_space=pltpu.SMEM),
                  cur(Q_DIM), prev(KV_DIM), cur(KV_DIM), prev(KV_DIM), cur(KV_DIM)],
        out_specs=cur(Q_DIM),
        out_shape=jax.ShapeDtypeStruct((batch * seq, Q_DIM), BF16),
        compiler_params=_cparams("arbitrary", "arbitrary"),
        name="swa_prompt",
    )(sinks, q, k, k, v, v)


def _swa_sample_kernel(sink_ref, q_ref, kc_ref, kn_ref, vc_ref, vn_ref, o_ref, ko_ref, vo_ref, *, nq, nb):
    c = WINDOW
    rows = GROUP * nq
    qi = lax.broadcasted_iota(jnp.int32, (rows, 2 * c), 0) % nq
    kj = lax.broadcasted_iota(jnp.int32, (rows, 2 * c), 1)
    mask = ((kj < c) & (kj > qi)) | ((kj >= c) & (kj - c <= qi))
    sinks = [jnp.concatenate([jnp.full((nq, 1), sink_ref[hd], F32)
                              for hd in range(kvh * GROUP, (kvh + 1) * GROUP)], axis=0)
             for kvh in range(N_KV_HEADS)]
    for b in range(nb):
        fill = jnp.zeros((c - nq, KV_DIM), F32)
        k2 = jnp.concatenate([kc_ref[b], kn_ref[b], fill], axis=0).astype(BF16)
        v2 = jnp.concatenate([vc_ref[b], vn_ref[b], fill], axis=0).astype(BF16)
        q = q_ref[b]
        outs = [None] * N_HEADS
        for kvh in range(N_KV_HEADS):
            heads = range(kvh * GROUP, (kvh + 1) * GROUP)
            qs = jnp.concatenate([q[:, hd * HEAD_DIM:(hd + 1) * HEAD_DIM] for hd in heads], axis=0).astype(BF16)
            s = lax.dot_general(qs, k2[:, kvh * HEAD_DIM:(kvh + 1) * HEAD_DIM], (((1,), (1,)), ((), ())),
                                preferred_element_type=F32)
            o = _sink_attend(s, mask, sinks[kvh], v2[:, kvh * HEAD_DIM:(kvh + 1) * HEAD_DIM])
            for g, hd in enumerate(heads):
                outs[hd] = o[g * nq:(g + 1) * nq, :]
        o_ref[b] = jnp.concatenate(outs, axis=1)
        for src_c, src_n, dst in ((kc_ref, kn_ref, ko_ref), (vc_ref, vn_ref, vo_ref)):
            dst[b, 0:c - nq, :] = src_c[b, nq:c, :]
            dst[b, c - nq:c, :] = src_n[b]


def _swa_sample(q, kc, kn, vc, vn, sinks):
    b_, nq, _ = q.shape
    assert nq <= WINDOW
    nb = _pick(b_, (8, 4, 2, 1))
    blk = lambda rows, cols: pl.BlockSpec((nb, rows, cols), lambda b: (b, 0, 0))
    return pl.pallas_call(
        functools.partial(_swa_sample_kernel, nq=nq, nb=nb),
        grid=(b_ // nb,),
        in_specs=[pl.BlockSpec(memory_space=pltpu.SMEM),
                  blk(nq, Q_DIM), blk(WINDOW, KV_DIM), blk(nq, KV_DIM), blk(WINDOW, KV_DIM), blk(nq, KV_DIM)],
        out_specs=[blk(nq, Q_DIM), blk(WINDOW, KV_DIM), blk(WINDOW, KV_DIM)],
        out_shape=[jax.ShapeDtypeStruct((b_, nq, Q_DIM), F32),
                   jax.ShapeDtypeStruct((b_, WINDOW, KV_DIM), F32),
                   jax.ShapeDtypeStruct((b_, WINDOW, KV_DIM), F32)],
        compiler_params=_cparams("arbitrary"),
        name="swa_sample",
    )(sinks, q, kc, kn, vc, vn)


def _rope_tables(pos):
    half = HEAD_DIM // 2
    inv = ROPE_THETA ** (-jnp.arange(half, dtype=F32) / half)
    ang = pos.astype(F32)[:, None] * inv[None, :]
    cos = jnp.cos(ang)
    sin = jnp.sin(ang)
    reps = V7X_LANES // HEAD_DIM
    cos_t = jnp.tile(jnp.concatenate([cos, cos], axis=1), (1, reps))
    sin_t = jnp.tile(jnp.concatenate([-sin, sin], axis=1), (1, reps))
    return cos_t, sin_t


def _pick(n, cands):
    for c in cands:
        if n % c == 0:
            return c
    raise ValueError(f"no tile for {n}")


def _peer(x1, pw, gf, attn=None, *, final_norm):
    n_ = x1.shape[0]
    x1, front = _peer_front(x1, pw, attn, tm=_pick(n_, (256, 128)))
    return _peer_dense(x1, front, pw, gf, tm=_pick(n_, (512, 256, 128)), eb=1024, final_norm=final_norm)


def kernel(x_prompt, x_sample, state_conv, state_h, cache_k, cache_v, rg_norm, rg_w_in, rg_conv_w,
           rg_conv_b, rg_wa, rg_ba, rg_wx, rg_bx, rg_lambda, rg_w_out, kv_norm, w_kv, attn_norm, w_q,
           sinks, w_o, ffn_norm, peer_wq, peer_subkeys, peer_u, peer_v, final_norm):
    d = D_MODEL
    row = lambda a: a.reshape(1, -1).astype(F32)
    rgw = dict(g=row(rg_norm[0]), w_in=rg_w_in[0].astype(BF16), conv_w=rg_conv_w[0],
               conv_b=row(rg_conv_b[0]), wa=rg_wa[0].astype(BF16), ba=row(rg_ba[0]),
               wx=rg_wx[0].astype(BF16), bx=row(rg_bx[0]), lam=row(rg_lambda[0]),
               w_out=rg_w_out[0].astype(BF16))
    u_all = peer_u.astype(BF16)
    vt_all = jnp.swapaxes(peer_v.astype(BF16), 1, 2)
    peer_w = []
    for l in range(2):
        wq = peer_wq[l].reshape(d, PEER_HEADS, 2, D_KEY // 2).transpose(2, 1, 3, 0).reshape(PQ_DIM, d)
        skx0 = jnp.einsum("nc,hg->nhgc", peer_subkeys[l, 0], jnp.eye(PEER_HEADS, dtype=F32))
        skx0 = skx0.reshape(PEER_HEADS * N_KEYS, PQ_DIM // 2)
        peer_w.append(dict(g=row(ffn_norm[l]), wqt=wq.astype(BF16), skx0=skx0.astype(BF16),
                           sk1=peer_subkeys[l, 1].astype(BF16), u=u_all[l], vt=vt_all[l]))
    attw = dict(g_kv=row(kv_norm), w_kv=w_kv.astype(BF16), g_q=row(attn_norm[0]), w_q=w_q[0].astype(BF16))
    wo = w_o[0].astype(BF16)
    sink = sinks[0].astype(F32)
    gf = row(final_norm)

    bp, tp, _ = x_prompt.shape
    steps_p = _pick(tp, (256, 128, 64, 32, 16, 8))
    pad_p = V7X_SUBLANES
    x1, conv_p, h_p = _rg_layer(x_prompt, jnp.zeros((bp, pad_p, d), F32), jnp.zeros((bp, 1, d), F32),
                                rgw, stride=1, steps=steps_p)
    xp = _peer(x1.reshape(bp * tp, d), peer_w[0], gf, final_norm=False)
    cos_p, sin_p = _rope_tables(jnp.arange(tp))
    tm_p = _pick(tp, (512, 256, 128))
    q, k, v = _qkv(xp, attw, cos_p, sin_p, tm=tm_p, table_blocks=tp // tm_p, q_dtype=BF16)
    o = _swa_prompt(q, k, v, sink, batch=bp, seq=tp)
    y_prompt = _peer(xp, peer_w[1], gf, attn=(o, wo), final_norm=True).reshape(bp, tp, d)
    k_p = k.reshape(bp, tp, N_KV_HEADS, HEAD_DIM)[:, -WINDOW:]
    v_p = v.reshape(bp, tp, N_KV_HEADS, HEAD_DIM)[:, -WINDOW:]
    conv_p = conv_p[None]
    h_p = h_p.reshape(1, bp, d)

    bs, ts, _ = x_sample.shape
    xs = x_sample.transpose(1, 0, 2).reshape(1, ts * bs, d)
    conv0 = state_conv[0].transpose(1, 0, 2).reshape(1, (CONV_W - 1) * bs, d)
    x1s, conv_s, h_s = _rg_layer(xs, conv0, state_h[0][None], rgw, stride=bs, steps=ts)
    xs1 = _peer(x1s.reshape(ts * bs, d), peer_w[0], gf, final_norm=False)
    xs1 = xs1.reshape(ts, bs, d).transpose(1, 0, 2).reshape(bs * ts, d)
    cos_s, sin_s = _rope_tables(PAST_LEN + jnp.arange(ts))
    cos_s = jnp.tile(cos_s, (bs, 1))
    sin_s = jnp.tile(sin_s, (bs, 1))
    tm_s = _pick(bs * ts, (512, 256, 128, 64, 32, 16, 8))
    qs, ks, vs = _qkv(xs1, attw, cos_s, sin_s, tm=tm_s, table_blocks=(bs * ts) // tm_s, q_dtype=F32)
    ck = cache_k.reshape(bs, WINDOW, KV_DIM)
    cv = cache_v.reshape(bs, WINDOW, KV_DIM)
    kn = ks.reshape(bs, ts, KV_DIM)
    vn = vs.reshape(bs, ts, KV_DIM)
    os_, k_s, v_s = _swa_sample(qs.reshape(bs, ts, Q_DIM), ck, kn, cv, vn, sink)
    y_sample = _peer(xs1, peer_w[1], gf, attn=(os_.reshape(bs * ts, Q_DIM), wo),
                     final_norm=True).reshape(bs, ts, d)
    k_s = k_s.reshape(bs, WINDOW, N_KV_HEADS, HEAD_DIM)
    v_s = v_s.reshape(bs, WINDOW, N_KV_HEADS, HEAD_DIM)
    conv_s = conv_s.reshape(CONV_W - 1, bs, d).transpose(1, 0, 2)[None]
    h_s = h_s.reshape(1, bs, d)

    return (y_prompt, y_sample, conv_p, h_p, k_p, v_p, conv_s, h_s, k_s, v_s)
```

```python
import functools
import math

import jax
import jax.numpy as jnp
from jax import lax
from jax.experimental import pallas as pl
from jax.experimental.pallas import tpu as pltpu

D_MODEL = 1024
PAST_LEN = 16384
D_RNN = D_MODEL
RG_BLOCKS = 8
RG_BLOCK_W = D_RNN // RG_BLOCKS
CONV_W = 4
RG_C = 8.0
HEAD_DIM = 64
N_HEADS = D_MODEL // HEAD_DIM
N_KV_HEADS = 4
GROUP = N_HEADS // N_KV_HEADS
WINDOW = 128
ROPE_THETA = 10000.0
N_KEYS = 128
N_EXPERTS = N_KEYS * N_KEYS
PEER_HEADS = 8
PEER_TOPK = 16
D_KEY = 256
EPS = 1e-6

KV_DIM = N_KV_HEADS * HEAD_DIM
Q_DIM = N_HEADS * HEAD_DIM
PQ_DIM = PEER_HEADS * D_KEY

V7X_LANES = 128
V7X_SUBLANES = 8
V7X_VMEM_LIMIT_BYTES = 56 * 1024 * 1024

BF16 = jnp.bfloat16
F32 = jnp.float32


def _cparams(*sem, flags=None):
    return pltpu.CompilerParams(dimension_semantics=sem, vmem_limit_bytes=V7X_VMEM_LIMIT_BYTES, flags=flags)


def _rms(x, g):
    return x * lax.rsqrt(jnp.mean(x * x, axis=-1, keepdims=True) + EPS) * g


def _gelu_tanh(x):
    u2 = (-2.0 * math.sqrt(2.0 / math.pi)) * (x * (1.0 + 0.044715 * (x * x)))
    return x / (1.0 + jnp.exp(u2))


def _sigmoid(x):
    return 1.0 / (1.0 + jnp.exp(-x))


def _rg_kernel(x_ref, conv0_ref, h0_ref, g_ref, win_ref, cw_ref, cb_ref, wa_ref, ba_ref,
               wx_ref, bx_ref, lam_ref, wout_ref,
               y_ref, conv_ref, hlast_ref,
               xbuf_ref, hcar_ref, *, stride, steps, pad):
    rows = stride * steps
    tail = (CONV_W - 1) * stride
    c = pl.program_id(1)

    @pl.when(c == 0)
    def _():
        xbuf_ref[0:pad, :] = conv0_ref[0]
        hcar_ref[...] = h0_ref[0]

    x = x_ref[0]
    xn = _rms(x, g_ref[...]).astype(BF16)
    proj = jnp.dot(xn, win_ref[...], preferred_element_type=F32)
    gate = _gelu_tanh(proj[:, :D_RNN])
    xr = proj[:, D_RNN:]
    xbuf_ref[pad:pad + rows, :] = xr

    y = cb_ref[...] + cw_ref[0:1, :] * xbuf_ref[pad - 3 * stride:pad - 3 * stride + rows, :]
    y = y + cw_ref[1:2, :] * xbuf_ref[pad - 2 * stride:pad - 2 * stride + rows, :]
    y = y + cw_ref[2:3, :] * xbuf_ref[pad - stride:pad - stride + rows, :]
    xc = y + cw_ref[3:4, :] * xr

    conv_ref[0] = xbuf_ref[pad + rows - tail:pad + rows, :]
    xbuf_ref[0:pad, :] = xbuf_ref[rows:rows + pad, :]

    xcb = xc.astype(BF16)
    rs, is_ = [], []
    for n in range(RG_BLOCKS):
        blk = xcb[:, n * RG_BLOCK_W:(n + 1) * RG_BLOCK_W]
        rs.append(jnp.dot(blk, wa_ref[n], preferred_element_type=F32))
        is_.append(jnp.dot(blk, wx_ref[n], preferred_element_type=F32))
    r = _sigmoid(jnp.concatenate(rs, axis=1) + ba_ref[...])
    i = _sigmoid(jnp.concatenate(is_, axis=1) + bx_ref[...])

    nlam = -lam_ref[...]
    softplus = jnp.maximum(nlam, 0.0) + jnp.log1p(jnp.exp(-jnp.abs(nlam)))
    log_a = -RG_C * r * softplus
    a = jnp.exp(log_a)
    mult = jnp.sqrt(-jnp.tanh(log_a) * (a * a + 1.0))
    b = mult * i * xc

    group = V7X_SUBLANES if (stride == 1 and steps % V7X_SUBLANES == 0) else steps
    row = lax.broadcasted_iota(jnp.int32, (rows, D_RNN), 0) % (group * stride)
    s = 1
    while s < group:
        sh = s * stride
        a_sh = pltpu.roll(a, sh, axis=0)
        b_sh = pltpu.roll(b, sh, axis=0)
        m = row >= sh
        b = jnp.where(m, a * b_sh + b, b)
        a = jnp.where(m, a * a_sh, a)
        s *= 2
    grows = group * stride
    carry = hcar_ref[...]
    hs = []
    for g0 in range(0, rows, grows):
        cg = carry if stride == 1 else jnp.concatenate([carry] * group, axis=0)
        hg_ = a[g0:g0 + grows, :] * cg + b[g0:g0 + grows, :]
        hs.append(hg_)
        carry = hg_[grows - stride:grows, :]
    h = hs[0] if len(hs) == 1 else jnp.concatenate(hs, axis=0)
    hl = carry
    hcar_ref[...] = hl
    hlast_ref[0] = hl

    hg = (h * gate).astype(BF16)
    y_ref[0] = x + jnp.dot(hg, wout_ref[...], preferred_element_type=F32)


def _rg_layer(x3, conv0, h0, w, *, stride, steps):
    g_, r_, d_ = x3.shape
    rows = stride * steps
    assert r_ % rows == 0
    pad = conv0.shape[1]
    tail = (CONV_W - 1) * stride
    nchunks = r_ // rows
    const2 = lambda g, c: (0, 0)
    const3 = lambda g, c: (0, 0, 0)
    kern = functools.partial(_rg_kernel, stride=stride, steps=steps, pad=pad)
    return pl.pallas_call(
        kern,
        grid=(g_, nchunks),
        in_specs=[
            pl.BlockSpec((1, rows, d_), lambda g, c: (g, c, 0)),
            pl.BlockSpec((1, pad, d_), lambda g, c: (g, 0, 0)),
            pl.BlockSpec((1, stride, d_), lambda g, c: (g, 0, 0)),
            pl.BlockSpec((1, d_), const2),
            pl.BlockSpec((d_, 2 * D_RNN), const2),
            pl.BlockSpec((CONV_W, D_RNN), const2),
            pl.BlockSpec((1, D_RNN), const2),
            pl.BlockSpec((RG_BLOCKS, RG_BLOCK_W, RG_BLOCK_W), const3),
            pl.BlockSpec((1, D_RNN), const2),
            pl.BlockSpec((RG_BLOCKS, RG_BLOCK_W, RG_BLOCK_W), const3),
            pl.BlockSpec((1, D_RNN), const2),
            pl.BlockSpec((1, D_RNN), const2),
            pl.BlockSpec((D_RNN, d_), const2),
        ],
        out_specs=[
            pl.BlockSpec((1, rows, d_), lambda g, c: (g, c, 0)),
            pl.BlockSpec((1, tail, D_RNN), lambda g, c: (g, 0, 0)),
            pl.BlockSpec((1, stride, D_RNN), lambda g, c: (g, 0, 0)),
        ],
        out_shape=[
            jax.ShapeDtypeStruct((g_, r_, d_), F32),
            jax.ShapeDtypeStruct((g_, tail, D_RNN), F32),
            jax.ShapeDtypeStruct((g_, stride, D_RNN), F32),
        ],
        scratch_shapes=[
            pltpu.VMEM((pad + rows, D_RNN), F32),
            pltpu.VMEM((stride, D_RNN), F32),
        ],
        compiler_params=_cparams("arbitrary", "arbitrary"),
        name="rg_layer",
    )(x3, conv0, h0, w["g"], w["w_in"], w["conv_w"], w["conv_b"], w["wa"], w["ba"],
      w["wx"], w["bx"], w["lam"], w["w_out"])


def _ce(v, i, j):
    hi = jnp.maximum(v[i], v[j])
    lo = jnp.minimum(v[i], v[j])
    v[i], v[j] = hi, lo


def _sort16(v):
    n = 16
    k = 2
    while k <= n:
        j = k // 2
        while j >= 1:
            for i in range(n):
                l = i ^ j
                if l > i:
                    if (i & k) == 0:
                        _ce(v, i, l)
                    else:
                        _ce(v, l, i)
            j //= 2
        k *= 2


def _merge_top16(a, b):
    v = [jnp.maximum(a[k], b[15 - k]) for k in range(16)]
    j = 8
    while j >= 1:
        for i in range(16):
            l = i ^ j
            if l > i:
                _ce(v, i, l)
        j //= 2
    return v


def _top16(vs):
    groups = []
    for g in range(len(vs) // 16):
        lst = list(vs[16 * g:16 * g + 16])
        _sort16(lst)
        groups.append(lst)
    while len(groups) > 1:
        groups = [_merge_top16(groups[i], groups[i + 1]) for i in range(0, len(groups), 2)]
    return groups[0]


def _kth_product(sv0, sv1, mul):
    cands = [mul(sv0[0], sv1[j]) for j in range(PEER_TOPK)]
    for i in range(1, PEER_TOPK):
        for j in range(PEER_TOPK // (i + 1)):
            cands.append(mul(sv0[i], sv1[j]))
    padded = list(cands)
    while len(padded) % 16:
        padded.append(jnp.full_like(cands[0], -1.0))
    return _top16(padded)[PEER_TOPK - 1], cands


def _peer_select(sc_ref, lt):
    sv0 = _top16([sc_ref[0, lt, n * PEER_HEADS:(n + 1) * PEER_HEADS, :] for n in range(N_KEYS)])
    sv1 = _top16([sc_ref[1, lt, pl.ds(n, PEER_HEADS, stride=N_KEYS), :] for n in range(N_KEYS)])
    theta, cands = _kth_product(sv0, sv1, lambda a, b: a * b)
    z = jnp.zeros_like(theta)
    for cnd in cands:
        z = z + jnp.where(cnd >= theta, cnd, 0.0)
    scale = 0.5 / z
    sv0n = [(v * scale).astype(BF16) for v in sv0]
    sv1b = [v.astype(BF16) for v in sv1]
    theta_n, _ = _kth_product(sv0n, sv1b, lambda a, b: (a * b).astype(F32))
    return scale, theta_n


def _peer_front_kernel(*refs, tm, with_attn):
    if with_attn:
        (x_ref, o_ref, wo_ref, g_ref, wqt_ref, skx_ref, sk_ref,
         x1_ref, xnt_ref, t0_ref, t1_ref, th_ref, sc_ref) = refs
        x1 = x_ref[...] + jnp.dot(o_ref[...].astype(BF16), wo_ref[...], preferred_element_type=F32)
        x1_ref[...] = x1
    else:
        (x_ref, g_ref, wqt_ref, skx_ref, sk_ref, xnt_ref, t0_ref, t1_ref, th_ref, sc_ref) = refs
        x1 = x_ref[...]
    nh = PEER_HEADS
    xn = _rms(x1, g_ref[...])
    xnt = xn.T.astype(BF16)
    xnt_ref[...] = xnt
    qt = jnp.dot(wqt_ref[...], xnt, preferred_element_type=F32).astype(BF16)
    s0 = jnp.dot(skx_ref[...], qt[0:PQ_DIM // 2, :], preferred_element_type=F32)
    mx = s0[0:nh, :]
    for n in range(1, N_KEYS):
        mx = jnp.maximum(mx, s0[n * nh:(n + 1) * nh, :])
    for n in range(N_KEYS):
        t = jnp.exp(s0[n * nh:(n + 1) * nh, :] - mx)
        for lt in range(tm // V7X_LANES):
            sc_ref[0, lt, n * nh:(n + 1) * nh, :] = t[:, lt * V7X_LANES:(lt + 1) * V7X_LANES]
    for h in range(nh):
        base = PQ_DIM // 2 + h * (D_KEY // 2)
        s = jnp.dot(sk_ref[...], qt[base:base + D_KEY // 2, :], preferred_element_type=F32)
        t = jnp.exp(s - jnp.max(s, axis=0, keepdims=True)).astype(BF16)
        t1_ref[h * N_KEYS:(h + 1) * N_KEYS, :] = t
        t = t.astype(F32)
        for lt in range(tm // V7X_LANES):
            sc_ref[1, lt, h * N_KEYS:(h + 1) * N_KEYS, :] = t[:, lt * V7X_LANES:(lt + 1) * V7X_LANES]
    for lt in range(tm // V7X_LANES):
        lanes = pl.ds(lt * V7X_LANES, V7X_LANES)
        scale, theta = _peer_select(sc_ref, lt)
        th_ref[:, lanes] = theta
        for n in range(N_KEYS):
            t0_ref[n * nh:(n + 1) * nh, lanes] = (
                sc_ref[0, lt, n * nh:(n + 1) * nh, :] * scale).astype(BF16).astype(F32)


def _peer_front(x, w, attn=None, *, tm):
    n_, d_ = x.shape
    assert n_ % tm == 0
    with_attn = attn is not None
    const2 = lambda i: (0, 0)
    tok = pl.BlockSpec((tm, d_), lambda i: (i, 0))
    in_specs = [tok]
    args = [x]
    if with_attn:
        in_specs += [pl.BlockSpec((tm, Q_DIM), lambda i: (i, 0)), pl.BlockSpec((Q_DIM, d_), const2)]
        args += list(attn)
    in_specs += [
        pl.BlockSpec((1, d_), const2),
        pl.BlockSpec((PQ_DIM, d_), const2),
        pl.BlockSpec((PEER_HEADS * N_KEYS, PQ_DIM // 2), const2),
        pl.BlockSpec((N_KEYS, D_KEY // 2), const2),
    ]
    args += [w["g"], w["wqt"], w["skx0"], w["sk1"]]
    feat = lambda rows: pl.BlockSpec((rows, tm), lambda i: (0, i))
    out_specs = [feat(d_), feat(PEER_HEADS * N_KEYS), feat(PEER_HEADS * N_KEYS), feat(PEER_HEADS)]
    out_shape = [
        jax.ShapeDtypeStruct((d_, n_), BF16),
        jax.ShapeDtypeStruct((PEER_HEADS * N_KEYS, n_), F32),
        jax.ShapeDtypeStruct((PEER_HEADS * N_KEYS, n_), BF16),
        jax.ShapeDtypeStruct((PEER_HEADS, n_), F32),
    ]
    if with_attn:
        out_specs = [tok] + out_specs
        out_shape = [jax.ShapeDtypeStruct((n_, d_), F32)] + out_shape
    outs = pl.pallas_call(
        functools.partial(_peer_front_kernel, tm=tm, with_attn=with_attn),
        grid=(n_ // tm,),
        in_specs=in_specs,
        out_specs=out_specs,
        out_shape=out_shape,
        scratch_shapes=[pltpu.VMEM((2, tm // V7X_LANES, PEER_HEADS * N_KEYS, V7X_LANES), F32)],
        compiler_params=_cparams("arbitrary"),
        name="peer_front_attn" if with_attn else "peer_front",
    )(*args)
    if with_attn:
        return outs[0], outs[1:]
    return x, outs


def _peer_weight_tile(lt, keys, ht_ref, pt_ref, t0_ref, t1_ref, th_ref):
    rb = 2 * V7X_SUBLANES
    lanes = slice(lt * V7X_LANES, (lt + 1) * V7X_LANES)
    bcast = lambda row: jnp.broadcast_to(row, (rb, V7X_LANES)).astype(BF16)
    th = [bcast(th_ref[h:h + 1, lanes]) for h in range(PEER_HEADS)]
    zero = jnp.zeros((rb, V7X_LANES), BF16)
    for al in keys:
        ta = [bcast(t0_ref[al * PEER_HEADS + h:al * PEER_HEADS + h + 1, lanes]) for h in range(PEER_HEADS)]
        for bg in range(N_KEYS // rb):
            w = None
            for h in range(PEER_HEADS):
                prod = ta[h] * t1_ref[h * N_KEYS + bg * rb:h * N_KEYS + (bg + 1) * rb, lanes]
                wh = jnp.where(prod >= th[h], prod, zero)
                w = wh if w is None else w + wh
            r0 = al * N_KEYS + bg * rb
            hv = ht_ref[r0:r0 + rb, lanes]
            gl = hv * (1.0 + lax.erf(hv * (1.0 / math.sqrt(2.0))))
            pt_ref[r0:r0 + rb, lanes] = gl.astype(BF16) * w


def _peer_half_step(xnt_ref, u_ref, u_rows, vt_ref, vt_cols, acc_ref, ht_w, pt_r, ht_r, pt_w,
                    t0_ref, t1_ref, th_ref, *, eb, tm):
    mxu_w = 2 * V7X_LANES
    pieces = []
    for c in range(tm // mxu_w):
        cols = slice(c * mxu_w, (c + 1) * mxu_w)

        def scores(cols=cols):
            ht_w[:, cols] = jnp.dot(u_ref[u_rows, :], xnt_ref[:, cols], preferred_element_type=F32)

        def values(cols=cols):
            acc_ref[:, cols] += jnp.dot(vt_ref[:, vt_cols], pt_r[:, cols], preferred_element_type=F32)

        pieces += [scores, values]
    for lt in range(tm // V7X_LANES):
        _peer_weight_tile(lt, range(eb // N_KEYS), ht_r, pt_w, t0_ref, t1_ref, th_ref)
        pieces[lt]()


def _peer_dense_kernel(xnt_ref, u_ref, vt_ref, t0a_ref, t1a_ref, tha_ref, t0b_ref, t1b_ref, thb_ref,
                       x1_ref, gf_ref, out_ref, acc_ref, ht0_ref, ht1_ref, pt0_ref, pt1_ref,
                       *, eb, ne, final_norm):
    g = pl.program_id(0)
    tm = acc_ref.shape[1]
    fv = 2 * g - 2

    @pl.when(g == 0)
    def _():
        ht1_ref[...] = jnp.zeros_like(ht1_ref)
        pt0_ref[...] = jnp.zeros_like(pt0_ref)

    @pl.when((g == 0) | (lax.rem(fv, ne) == 0))
    def _():
        acc_ref[...] = jnp.zeros_like(acc_ref)

    _peer_half_step(xnt_ref, u_ref, slice(0, eb), vt_ref, slice(0, eb), acc_ref,
                    ht0_ref, pt0_ref, ht1_ref, pt1_ref, t0a_ref, t1a_ref, tha_ref, eb=eb, tm=tm)
    _peer_half_step(xnt_ref, u_ref, slice(eb, 2 * eb), vt_ref, slice(eb, 2 * eb), acc_ref,
                    ht1_ref, pt1_ref, ht0_ref, pt0_ref, t0b_ref, t1b_ref, thb_ref, eb=eb, tm=tm)

    @pl.when((g > 0) & (lax.rem(fv + 1, ne) == ne - 1))
    def _():
        y = x1_ref[...] + acc_ref[...].T
        if final_norm:
            y = _rms(y, gf_ref[...])
        out_ref[...] = y


def _peer_dense(x1, front, w, gf, *, layer, tm, eb, final_norm):
    xnt, t0, t1, th = front
    n_, d_ = x1.shape
    ne = N_EXPERTS // eb
    assert n_ % tm == 0 and N_EXPERTS % eb == 0 and eb % N_KEYS == 0 and ne % 2 == 0
    assert tm % (2 * V7X_LANES) == 0
    nblocks = (n_ // tm) * ne
    last = nblocks - 1
    tok_mm1 = lambda g: jnp.minimum(2 * g, last) // ne
    tok_a = lambda g: jnp.clip(2 * g - 1, 0, last) // ne
    tok_v = lambda g: jnp.maximum(2 * g - 2, 0) // ne
    blk_a = lambda g: jnp.clip(2 * g - 1, 0, last) % ne
    blk_b = lambda g: jnp.minimum(2 * g, last) % ne
    pair_mm1 = lambda g: (jnp.minimum(2 * g, last) % ne) // 2
    pair_v = lambda g: (jnp.maximum(2 * g - 2, 0) % ne) // 2
    feat = lambda rows, tok: pl.BlockSpec((rows, tm), lambda g: (0, tok(g)))
    nk = PEER_HEADS * N_KEYS
    t0_rows = (eb // N_KEYS) * PEER_HEADS
    return pl.pallas_call(
        functools.partial(_peer_dense_kernel, eb=eb, ne=ne, final_norm=final_norm),
        grid=(nblocks // 2 + 1,),
        in_specs=[
            feat(d_, tok_mm1),
            pl.BlockSpec((None, 2 * eb, d_), lambda g: (layer, pair_mm1(g), 0)),
            pl.BlockSpec((None, d_, 2 * eb), lambda g: (layer, 0, pair_v(g))),
            pl.BlockSpec((t0_rows, tm), lambda g: (blk_a(g), tok_a(g))), feat(nk, tok_a), feat(PEER_HEADS, tok_a),
            pl.BlockSpec((t0_rows, tm), lambda g: (blk_b(g), tok_mm1(g))), feat(nk, tok_mm1),
            feat(PEER_HEADS, tok_mm1),
            pl.BlockSpec((tm, d_), lambda g: (tok_v(g), 0)),
            pl.BlockSpec((1, d_), lambda g: (0, 0)),
        ],
        out_specs=pl.BlockSpec((tm, d_), lambda g: (tok_v(g), 0)),
        out_shape=jax.ShapeDtypeStruct((n_, d_), F32),
        scratch_shapes=[
            pltpu.VMEM((d_, tm), F32),
            pltpu.VMEM((eb, tm), F32), pltpu.VMEM((eb, tm), F32),
            pltpu.VMEM((eb, tm), BF16), pltpu.VMEM((eb, tm), BF16),
        ],
        compiler_params=_cparams("arbitrary"),
        name="peer_dense",
    )(xnt, w["u"], w["vt"], t0, t1, th, t0, t1, th, x1, gf)


def _rope(x, cos, sin_signed):
    half = HEAD_DIM // 2
    lane = lax.broadcasted_iota(jnp.int32, (1, V7X_LANES), 1)
    first = (lane % HEAD_DIM) < half
    outs = []
    for t in range(x.shape[1] // V7X_LANES):
        xt = x[:, t * V7X_LANES:(t + 1) * V7X_LANES]
        swapped = jnp.where(first, pltpu.roll(xt, V7X_LANES - half, axis=1), pltpu.roll(xt, half, axis=1))
        outs.append(xt * cos + swapped * sin_signed)
    return jnp.concatenate(outs, axis=1)


def _qkv_kernel(x_ref, gkv_ref, wkv_ref, gq_ref, wq_ref, cos_ref, sin_ref, q_ref, k_ref, v_ref):
    x = x_ref[...]
    cos = cos_ref[...]
    sin = sin_ref[...]
    kv = jnp.dot(_rms(x, gkv_ref[...]).astype(BF16), wkv_ref[...], preferred_element_type=F32)
    k_ref[...] = _rope(kv[:, :KV_DIM], cos, sin)
    v_ref[...] = kv[:, KV_DIM:]
    q = jnp.dot(_rms(x, gq_ref[...]).astype(BF16), wq_ref[...], preferred_element_type=F32)
    q_ref[...] = (_rope(q, cos, sin) * (HEAD_DIM ** -0.5)).astype(q_ref.dtype)


def _qkv(x, w, cos, sin, *, tm, table_blocks, q_dtype):
    n_, d_ = x.shape
    assert n_ % tm == 0
    const2 = lambda i: (0, 0)
    tok = lambda cols: pl.BlockSpec((tm, cols), lambda i: (i, 0))
    tab = pl.BlockSpec((tm, V7X_LANES), lambda i: (i % table_blocks, 0))
    return pl.pallas_call(
        _qkv_kernel,
        grid=(n_ // tm,),
        in_specs=[tok(d_), pl.BlockSpec((1, d_), const2), pl.BlockSpec((d_, 2 * KV_DIM), const2),
                  pl.BlockSpec((1, d_), const2), pl.BlockSpec((d_, Q_DIM), const2), tab, tab],
        out_specs=[tok(Q_DIM), tok(KV_DIM), tok(KV_DIM)],
        out_shape=[jax.ShapeDtypeStruct((n_, Q_DIM), q_dtype),
                   jax.ShapeDtypeStruct((n_, KV_DIM), F32),
                   jax.ShapeDtypeStruct((n_, KV_DIM), F32)],
        compiler_params=_cparams("arbitrary"),
        name="qkv_rope",
    )(x, w["g_kv"], w["w_kv"], w["g_q"], w["w_q"], cos, sin)


def _sink_attend(s, mask, sink, v):
    s = jnp.where(mask, s, -jnp.inf)
    m = jnp.maximum(jnp.max(s, axis=-1, keepdims=True), sink)
    p = jnp.exp(s - m)
    denom = jnp.sum(p, axis=-1, keepdims=True) + jnp.exp(sink - m)
    return jnp.dot(p.astype(BF16), v, preferred_element_type=F32) / denom


def _swa_prompt_kernel(sink_ref, q_ref, kp_ref, kc_ref, vp_ref, vc_ref, o_ref):
    n = pl.program_id(1)
    c = WINDOW
    k2 = jnp.concatenate([kp_ref[...], kc_ref[...]], axis=0).astype(BF16)
    v2 = jnp.concatenate([vp_ref[...], vc_ref[...]], axis=0).astype(BF16)
    qi = lax.broadcasted_iota(jnp.int32, (c, 2 * c), 0)
    kj = lax.broadcasted_iota(jnp.int32, (c, 2 * c), 1)
    mask = (kj > qi) & (kj <= qi + c) & ((kj >= c) | (n > 0))
    outs = []
    for hd in range(N_HEADS):
        kvh = hd // GROUP
        q = q_ref[:, hd * HEAD_DIM:(hd + 1) * HEAD_DIM]
        kk = k2[:, kvh * HEAD_DIM:(kvh + 1) * HEAD_DIM]
        s = lax.dot_general(q, kk, (((1,), (1,)), ((), ())), preferred_element_type=F32)
        outs.append(_sink_attend(s, mask, sink_ref[hd], v2[:, kvh * HEAD_DIM:(kvh + 1) * HEAD_DIM]))
    o_ref[...] = jnp.concatenate(outs, axis=1).astype(BF16)


def _swa_prompt(q, k, v, sinks, *, batch, seq):
    nb = seq // WINDOW
    cur = lambda cols: pl.BlockSpec((WINDOW, cols), lambda b, n: (b * nb + n, 0))
    prev = lambda cols: pl.BlockSpec((WINDOW, cols), lambda b, n: (b * nb + jnp.maximum(n - 1, 0), 0))
    return pl.pallas_call(
        _swa_prompt_kernel,
        grid=(batch, nb),
        in_specs=[pl.BlockSpec(memory_space=pltpu.SMEM),
                  cur(Q_DIM), prev(KV_DIM), cur(KV_DIM), prev(KV_DIM), cur(KV_DIM)],
        out_specs=cur(Q_DIM),
        out_shape=jax.ShapeDtypeStruct((batch * seq, Q_DIM), BF16),
        compiler_params=_cparams("arbitrary", "arbitrary"),
        name="swa_prompt",
    )(sinks, q, k, k, v, v)


def _swa_sample_kernel(sink_ref, q_ref, kc_ref, kn_ref, vc_ref, vn_ref, o_ref, ko_ref, vo_ref, *, nq, nb):
    c = WINDOW
    rows = GROUP * nq
    qi = lax.broadcasted_iota(jnp.int32, (rows, 2 * c), 0) % nq
    kj = lax.broadcasted_iota(jnp.int32, (rows, 2 * c), 1)
    mask = ((kj < c) & (kj > qi)) | ((kj >= c) & (kj - c <= qi))
    sinks = [jnp.concatenate([jnp.full((nq, 1), sink_ref[hd], F32)
                              for hd in range(kvh * GROUP, (kvh + 1) * GROUP)], axis=0)
             for kvh in range(N_KV_HEADS)]
    for b in range(nb):
        fill = jnp.zeros((c - nq, KV_DIM), F32)
        k2 = jnp.concatenate([kc_ref[b], kn_ref[b], fill], axis=0).astype(BF16)
        v2 = jnp.concatenate([vc_ref[b], vn_ref[b], fill], axis=0).astype(BF16)
        q = q_ref[b]
        outs = [None] * N_HEADS
        for kvh in range(N_KV_HEADS):
            heads = range(kvh * GROUP, (kvh + 1) * GROUP)
            qs = jnp.concatenate([q[:, hd * HEAD_DIM:(hd + 1) * HEAD_DIM] for hd in heads], axis=0).astype(BF16)
            s = lax.dot_general(qs, k2[:, kvh * HEAD_DIM:(kvh + 1) * HEAD_DIM], (((1,), (1,)), ((), ())),
                                preferred_element_type=F32)
            o = _sink_attend(s, mask, sinks[kvh], v2[:, kvh * HEAD_DIM:(kvh + 1) * HEAD_DIM])
            for g, hd in enumerate(heads):
                outs[hd] = o[g * nq:(g + 1) * nq, :]
        o_ref[b] = jnp.concatenate(outs, axis=1)
        for src_c, src_n, dst in ((kc_ref, kn_ref, ko_ref), (vc_ref, vn_ref, vo_ref)):
            dst[b, 0:c - nq, :] = src_c[b, nq:c, :]
            dst[b, c - nq:c, :] = src_n[b]


def _swa_sample(q, kc, kn, vc, vn, sinks):
    b_, nq, _ = q.shape
    assert nq <= WINDOW
    nb = _pick(b_, (8, 4, 2, 1))
    blk = lambda rows, cols: pl.BlockSpec((nb, rows, cols), lambda b: (b, 0, 0))
    return pl.pallas_call(
        functools.partial(_swa_sample_kernel, nq=nq, nb=nb),
        grid=(b_ // nb,),
        in_specs=[pl.BlockSpec(memory_space=pltpu.SMEM),
                  blk(nq, Q_DIM), blk(WINDOW, KV_DIM), blk(nq, KV_DIM), blk(WINDOW, KV_DIM), blk(nq, KV_DIM)],
        out_specs=[blk(nq, Q_DIM), blk(WINDOW, KV_DIM), blk(WINDOW, KV_DIM)],
        out_shape=[jax.ShapeDtypeStruct((b_, nq, Q_DIM), F32),
                   jax.ShapeDtypeStruct((b_, WINDOW, KV_DIM), F32),
                   jax.ShapeDtypeStruct((b_, WINDOW, KV_DIM), F32)],
        compiler_params=_cparams("arbitrary"),
        name="swa_sample",
    )(sinks, q, kc, kn, vc, vn)


def _rope_tables(pos):
    half = HEAD_DIM // 2
    inv = ROPE_THETA ** (-jnp.arange(half, dtype=F32) / half)
    ang = pos.astype(F32)[:, None] * inv[None, :]
    cos = jnp.cos(ang)
    sin = jnp.sin(ang)
    reps = V7X_LANES // HEAD_DIM
    cos_t = jnp.tile(jnp.concatenate([cos, cos], axis=1), (1, reps))
    sin_t = jnp.tile(jnp.concatenate([-sin, sin], axis=1), (1, reps))
    return cos_t, sin_t


def _pick(n, cands):
    for c in cands:
        if n % c == 0:
            return c
    raise ValueError(f"no tile for {n}")


def _peer(x1, pw, gf, attn=None, *, layer, final_norm):
    n_ = x1.shape[0]
    x1, front = _peer_front(x1, pw, attn, tm=_pick(n_, (256, 128)))
    return _peer_dense(x1, front, pw, gf, layer=layer, tm=_pick(n_, (512, 256, 128)), eb=1024,
                       final_norm=final_norm)


def kernel(x_prompt, x_sample, state_conv, state_h, cache_k, cache_v, rg_norm, rg_w_in, rg_conv_w,
           rg_conv_b, rg_wa, rg_ba, rg_wx, rg_bx, rg_lambda, rg_w_out, kv_norm, w_kv, attn_norm, w_q,
           sinks, w_o, ffn_norm, peer_wq, peer_subkeys, peer_u, peer_v, final_norm):
    d = D_MODEL
    row = lambda a: a.reshape(1, -1).astype(F32)
    rgw = dict(g=row(rg_norm[0]), w_in=rg_w_in[0].astype(BF16), conv_w=rg_conv_w[0],
               conv_b=row(rg_conv_b[0]), wa=rg_wa[0].astype(BF16), ba=row(rg_ba[0]),
               wx=rg_wx[0].astype(BF16), bx=row(rg_bx[0]), lam=row(rg_lambda[0]),
               w_out=rg_w_out[0].astype(BF16))
    u_all = peer_u.astype(BF16)
    vt_all = jnp.swapaxes(peer_v.astype(BF16), 1, 2)
    peer_w = []
    for l in range(2):
        wq = peer_wq[l].reshape(d, PEER_HEADS, 2, D_KEY // 2).transpose(2, 1, 3, 0).reshape(PQ_DIM, d)
        skx0 = jnp.einsum("nc,hg->nhgc", peer_subkeys[l, 0], jnp.eye(PEER_HEADS, dtype=F32))
        skx0 = skx0.reshape(PEER_HEADS * N_KEYS, PQ_DIM // 2)
        peer_w.append(dict(g=row(ffn_norm[l]), wqt=wq.astype(BF16), skx0=skx0.astype(BF16),
                           sk1=peer_subkeys[l, 1].astype(BF16), u=u_all, vt=vt_all))
    attw = dict(g_kv=row(kv_norm), w_kv=w_kv.astype(BF16), g_q=row(attn_norm[0]), w_q=w_q[0].astype(BF16))
    wo = w_o[0].astype(BF16)
    sink = sinks[0].astype(F32)
    gf = row(final_norm)

    bp, tp, _ = x_prompt.shape
    steps_p = _pick(tp, (256, 128, 64, 32, 16, 8))
    pad_p = V7X_SUBLANES
    x1, conv_p, h_p = _rg_layer(x_prompt, jnp.zeros((bp, pad_p, d), F32), jnp.zeros((bp, 1, d), F32),
                                rgw, stride=1, steps=steps_p)
    xp = _peer(x1.reshape(bp * tp, d), peer_w[0], gf, layer=0, final_norm=False)
    cos_p, sin_p = _rope_tables(jnp.arange(tp))
    tm_p = _pick(tp, (512, 256, 128))
    q, k, v = _qkv(xp, attw, cos_p, sin_p, tm=tm_p, table_blocks=tp // tm_p, q_dtype=BF16)
    o = _swa_prompt(q, k, v, sink, batch=bp, seq=tp)
    y_prompt = _peer(xp, peer_w[1], gf, attn=(o, wo), layer=1, final_norm=True).reshape(bp, tp, d)
    k_p = k.reshape(bp, tp, N_KV_HEADS, HEAD_DIM)[:, -WINDOW:]
    v_p = v.reshape(bp, tp, N_KV_HEADS, HEAD_DIM)[:, -WINDOW:]
    conv_p = conv_p[None]
    h_p = h_p.reshape(1, bp, d)

    bs, ts, _ = x_sample.shape
    xs = x_sample.transpose(1, 0, 2).reshape(1, ts * bs, d)
    conv0 = state_conv[0].transpose(1, 0, 2).reshape(1, (CONV_W - 1) * bs, d)
    x1s, conv_s, h_s = _rg_layer(xs, conv0, state_h[0][None], rgw, stride=bs, steps=ts)
    xs1 = _peer(x1s.reshape(ts * bs, d), peer_w[0], gf, layer=0, final_norm=False)
    xs1 = xs1.reshape(ts, bs, d).transpose(1, 0, 2).reshape(bs * ts, d)
    cos_s, sin_s = _rope_tables(PAST_LEN + jnp.arange(ts))
    cos_s = jnp.tile(cos_s, (bs, 1))
    sin_s = jnp.tile(sin_s, (bs, 1))
    tm_s = _pick(bs * ts, (512, 256, 128, 64, 32, 16, 8))
    qs, ks, vs = _qkv(xs1, attw, cos_s, sin_s, tm=tm_s, table_blocks=(bs * ts) // tm_s, q_dtype=F32)
    ck = cache_k.reshape(bs, WINDOW, KV_DIM)
    cv = cache_v.reshape(bs, WINDOW, KV_DIM)
    kn = ks.reshape(bs, ts, KV_DIM)
    vn = vs.reshape(bs, ts, KV_DIM)
    os_, k_s, v_s = _swa_sample(qs.reshape(bs, ts, Q_DIM), ck, kn, cv, vn, sink)
    y_sample = _peer(xs1, peer_w[1], gf, attn=(os_.reshape(bs * ts, Q_DIM), wo), layer=1,
                     final_norm=True).reshape(bs, ts, d)
    k_s = k_s.reshape(bs, WINDOW, N_KV_HEADS, HEAD_DIM)
    v_s = v_s.reshape(bs, WINDOW, N_KV_HEADS, HEAD_DIM)
    conv_s = conv_s.reshape(CONV_W - 1, bs, d).transpose(1, 0, 2)[None]
    h_s = h_s.reshape(1, bs, d)

    return (y_prompt, y_sample, conv_p, h_p, k_p, v_p, conv_s, h_s, k_s, v_s)
```

```python
import functools
import math

import jax
import jax.numpy as jnp
from jax import lax
from jax.experimental import pallas as pl
from jax.experimental.pallas import tpu as pltpu

D_MODEL = 1024
PAST_LEN = 16384
D_RNN = D_MODEL
RG_BLOCKS = 8
RG_BLOCK_W = D_RNN // RG_BLOCKS
CONV_W = 4
RG_C = 8.0
HEAD_DIM = 64
N_HEADS = D_MODEL // HEAD_DIM
N_KV_HEADS = 4
GROUP = N_HEADS // N_KV_HEADS
WINDOW = 128
ROPE_THETA = 10000.0
N_KEYS = 128
N_EXPERTS = N_KEYS * N_KEYS
PEER_HEADS = 8
PEER_TOPK = 16
D_KEY = 256
EPS = 1e-6

KV_DIM = N_KV_HEADS * HEAD_DIM
Q_DIM = N_HEADS * HEAD_DIM
PQ_DIM = PEER_HEADS * D_KEY

V7X_LANES = 128
V7X_SUBLANES = 8
V7X_VMEM_LIMIT_BYTES = 56 * 1024 * 1024

BF16 = jnp.bfloat16
F32 = jnp.float32


def _cparams(*sem, flags=None):
    return pltpu.CompilerParams(dimension_semantics=sem, vmem_limit_bytes=V7X_VMEM_LIMIT_BYTES, flags=flags)


def _rms(x, g):
    return x * lax.rsqrt(jnp.mean(x * x, axis=-1, keepdims=True) + EPS) * g


def _gelu_tanh(x):
    u2 = (-2.0 * math.sqrt(2.0 / math.pi)) * (x * (1.0 + 0.044715 * (x * x)))
    return x / (1.0 + jnp.exp(u2))


def _sigmoid(x):
    return 1.0 / (1.0 + jnp.exp(-x))


def _rg_kernel(x_ref, conv0_ref, h0_ref, g_ref, win_ref, cw_ref, cb_ref, wa_ref, ba_ref,
               wx_ref, bx_ref, lam_ref, wout_ref,
               y_ref, conv_ref, hlast_ref,
               xbuf_ref, hcar_ref, *, stride, steps, pad):
    rows = stride * steps
    tail = (CONV_W - 1) * stride
    c = pl.program_id(1)

    @pl.when(c == 0)
    def _():
        xbuf_ref[0:pad, :] = conv0_ref[0]
        hcar_ref[...] = h0_ref[0]

    x = x_ref[0]
    xn = _rms(x, g_ref[...]).astype(BF16)
    proj = jnp.dot(xn, win_ref[...], preferred_element_type=F32)
    gate = _gelu_tanh(proj[:, :D_RNN])
    xr = proj[:, D_RNN:]
    xbuf_ref[pad:pad + rows, :] = xr

    y = cb_ref[...] + cw_ref[0:1, :] * xbuf_ref[pad - 3 * stride:pad - 3 * stride + rows, :]
    y = y + cw_ref[1:2, :] * xbuf_ref[pad - 2 * stride:pad - 2 * stride + rows, :]
    y = y + cw_ref[2:3, :] * xbuf_ref[pad - stride:pad - stride + rows, :]
    xc = y + cw_ref[3:4, :] * xr

    conv_ref[0] = xbuf_ref[pad + rows - tail:pad + rows, :]
    xbuf_ref[0:pad, :] = xbuf_ref[rows:rows + pad, :]

    xcb = xc.astype(BF16)
    rs, is_ = [], []
    for n in range(RG_BLOCKS):
        blk = xcb[:, n * RG_BLOCK_W:(n + 1) * RG_BLOCK_W]
        rs.append(jnp.dot(blk, wa_ref[n], preferred_element_type=F32))
        is_.append(jnp.dot(blk, wx_ref[n], preferred_element_type=F32))
    r = _sigmoid(jnp.concatenate(rs, axis=1) + ba_ref[...])
    i = _sigmoid(jnp.concatenate(is_, axis=1) + bx_ref[...])

    nlam = -lam_ref[...]
    softplus = jnp.maximum(nlam, 0.0) + jnp.log1p(jnp.exp(-jnp.abs(nlam)))
    log_a = -RG_C * r * softplus
    a = jnp.exp(log_a)
    mult = jnp.sqrt(-jnp.tanh(log_a) * (a * a + 1.0))
    b = mult * i * xc

    group = V7X_SUBLANES if (stride == 1 and steps % V7X_SUBLANES == 0) else steps
    row = lax.broadcasted_iota(jnp.int32, (rows, D_RNN), 0) % (group * stride)
    s = 1
    while s < group:
        sh = s * stride
        a_sh = pltpu.roll(a, sh, axis=0)
        b_sh = pltpu.roll(b, sh, axis=0)
        m = row >= sh
        b = jnp.where(m, a * b_sh + b, b)
        a = jnp.where(m, a * a_sh, a)
        s *= 2
    grows = group * stride
    carry = hcar_ref[...]
    hs = []
    for g0 in range(0, rows, grows):
        cg = carry if stride == 1 else jnp.concatenate([carry] * group, axis=0)
        hg_ = a[g0:g0 + grows, :] * cg + b[g0:g0 + grows, :]
        hs.append(hg_)
        carry = hg_[grows - stride:grows, :]
    h = hs[0] if len(hs) == 1 else jnp.concatenate(hs, axis=0)
    hl = carry
    hcar_ref[...] = hl
    hlast_ref[0] = hl

    hg = (h * gate).astype(BF16)
    y_ref[0] = x + jnp.dot(hg, wout_ref[...], preferred_element_type=F32)


def _rg_layer(x3, conv0, h0, w, *, stride, steps):
    g_, r_, d_ = x3.shape
    rows = stride * steps
    assert r_ % rows == 0
    pad = conv0.shape[1]
    tail = (CONV_W - 1) * stride
    nchunks = r_ // rows
    const2 = lambda g, c: (0, 0)
    const3 = lambda g, c: (0, 0, 0)
    kern = functools.partial(_rg_kernel, stride=stride, steps=steps, pad=pad)
    return pl.pallas_call(
        kern,
        grid=(g_, nchunks),
        in_specs=[
            pl.BlockSpec((1, rows, d_), lambda g, c: (g, c, 0)),
            pl.BlockSpec((1, pad, d_), lambda g, c: (g, 0, 0)),
            pl.BlockSpec((1, stride, d_), lambda g, c: (g, 0, 0)),
            pl.BlockSpec((1, d_), const2),
            pl.BlockSpec((d_, 2 * D_RNN), const2),
            pl.BlockSpec((CONV_W, D_RNN), const2),
            pl.BlockSpec((1, D_RNN), const2),
            pl.BlockSpec((RG_BLOCKS, RG_BLOCK_W, RG_BLOCK_W), const3),
            pl.BlockSpec((1, D_RNN), const2),
            pl.BlockSpec((RG_BLOCKS, RG_BLOCK_W, RG_BLOCK_W), const3),
            pl.BlockSpec((1, D_RNN), const2),
            pl.BlockSpec((1, D_RNN), const2),
            pl.BlockSpec((D_RNN, d_), const2),
        ],
        out_specs=[
            pl.BlockSpec((1, rows, d_), lambda g, c: (g, c, 0)),
            pl.BlockSpec((1, tail, D_RNN), lambda g, c: (g, 0, 0)),
            pl.BlockSpec((1, stride, D_RNN), lambda g, c: (g, 0, 0)),
        ],
        out_shape=[
            jax.ShapeDtypeStruct((g_, r_, d_), F32),
            jax.ShapeDtypeStruct((g_, tail, D_RNN), F32),
            jax.ShapeDtypeStruct((g_, stride, D_RNN), F32),
        ],
        scratch_shapes=[
            pltpu.VMEM((pad + rows, D_RNN), F32),
            pltpu.VMEM((stride, D_RNN), F32),
        ],
        compiler_params=_cparams("arbitrary", "arbitrary"),
        name="rg_layer",
    )(x3, conv0, h0, w["g"], w["w_in"], w["conv_w"], w["conv_b"], w["wa"], w["ba"],
      w["wx"], w["bx"], w["lam"], w["w_out"])


def _ce(v, i, j):
    hi = jnp.maximum(v[i], v[j])
    lo = jnp.minimum(v[i], v[j])
    v[i], v[j] = hi, lo


def _sort16(v):
    n = 16
    k = 2
    while k <= n:
        j = k // 2
        while j >= 1:
            for i in range(n):
                l = i ^ j
                if l > i:
                    if (i & k) == 0:
                        _ce(v, i, l)
                    else:
                        _ce(v, l, i)
            j //= 2
        k *= 2


def _merge_top16(a, b):
    v = [jnp.maximum(a[k], b[15 - k]) for k in range(16)]
    j = 8
    while j >= 1:
        for i in range(16):
            l = i ^ j
            if l > i:
                _ce(v, i, l)
        j //= 2
    return v


def _top16(vs):
    groups = []
    for g in range(len(vs) // 16):
        lst = list(vs[16 * g:16 * g + 16])
        _sort16(lst)
        groups.append(lst)
    while len(groups) > 1:
        groups = [_merge_top16(groups[i], groups[i + 1]) for i in range(0, len(groups), 2)]
    return groups[0]


def _kth_product(sv0, sv1, mul):
    cands = [mul(sv0[0], sv1[j]) for j in range(PEER_TOPK)]
    for i in range(1, PEER_TOPK):
        for j in range(PEER_TOPK // (i + 1)):
            cands.append(mul(sv0[i], sv1[j]))
    padded = list(cands)
    while len(padded) % 16:
        padded.append(jnp.full_like(cands[0], -1.0))
    return _top16(padded)[PEER_TOPK - 1], cands


def _peer_select(sc_ref, lt):
    sv0 = _top16([sc_ref[0, lt, n * PEER_HEADS:(n + 1) * PEER_HEADS, :] for n in range(N_KEYS)])
    sv1 = _top16([sc_ref[1, lt, pl.ds(n, PEER_HEADS, stride=N_KEYS), :] for n in range(N_KEYS)])
    theta, cands = _kth_product(sv0, sv1, lambda a, b: a * b)
    z = jnp.zeros_like(theta)
    for cnd in cands:
        z = z + jnp.where(cnd >= theta, cnd, 0.0)
    scale = 0.5 / z
    sv0n = [(v * scale).astype(BF16) for v in sv0]
    sv1b = [v.astype(BF16) for v in sv1]
    theta_n, _ = _kth_product(sv0n, sv1b, lambda a, b: (a * b).astype(F32))
    return scale, theta_n


def _peer_front_kernel(*refs, tm, with_attn):
    if with_attn:
        (x_ref, o_ref, wo_ref, g_ref, wqt_ref, skx_ref, sk_ref,
         x1_ref, xnt_ref, t0_ref, t1_ref, th_ref, sc_ref) = refs
        x1 = x_ref[...] + jnp.dot(o_ref[...].astype(BF16), wo_ref[...], preferred_element_type=F32)
        x1_ref[...] = x1
    else:
        (x_ref, g_ref, wqt_ref, skx_ref, sk_ref, xnt_ref, t0_ref, t1_ref, th_ref, sc_ref) = refs
        x1 = x_ref[...]
    nh = PEER_HEADS
    xn = _rms(x1, g_ref[...])
    xnt = xn.T.astype(BF16)
    xnt_ref[...] = xnt
    qt = jnp.dot(wqt_ref[...], xnt, preferred_element_type=F32).astype(BF16)
    s0 = jnp.dot(skx_ref[...], qt[0:PQ_DIM // 2, :], preferred_element_type=F32)
    mx = s0[0:nh, :]
    for n in range(1, N_KEYS):
        mx = jnp.maximum(mx, s0[n * nh:(n + 1) * nh, :])
    for n in range(N_KEYS):
        t = jnp.exp(s0[n * nh:(n + 1) * nh, :] - mx)
        for lt in range(tm // V7X_LANES):
            sc_ref[0, lt, n * nh:(n + 1) * nh, :] = t[:, lt * V7X_LANES:(lt + 1) * V7X_LANES]
    for h in range(nh):
        base = PQ_DIM // 2 + h * (D_KEY // 2)
        s = jnp.dot(sk_ref[...], qt[base:base + D_KEY // 2, :], preferred_element_type=F32)
        t = jnp.exp(s - jnp.max(s, axis=0, keepdims=True)).astype(BF16)
        t1_ref[h * N_KEYS:(h + 1) * N_KEYS, :] = t
        t = t.astype(F32)
        for lt in range(tm // V7X_LANES):
            sc_ref[1, lt, h * N_KEYS:(h + 1) * N_KEYS, :] = t[:, lt * V7X_LANES:(lt + 1) * V7X_LANES]
    for lt in range(tm // V7X_LANES):
        lanes = pl.ds(lt * V7X_LANES, V7X_LANES)
        scale, theta = _peer_select(sc_ref, lt)
        th_ref[:, lanes] = theta
        for n in range(N_KEYS):
            t0_ref[n * nh:(n + 1) * nh, lanes] = (
                sc_ref[0, lt, n * nh:(n + 1) * nh, :] * scale).astype(BF16).astype(F32)


def _peer_front(x, w, attn=None, *, tm):
    n_, d_ = x.shape
    assert n_ % tm == 0
    with_attn = attn is not None
    const2 = lambda i: (0, 0)
    tok = pl.BlockSpec((tm, d_), lambda i: (i, 0))
    in_specs = [tok]
    args = [x]
    if with_attn:
        in_specs += [pl.BlockSpec((tm, Q_DIM), lambda i: (i, 0)), pl.BlockSpec((Q_DIM, d_), const2)]
        args += list(attn)
    in_specs += [
        pl.BlockSpec((1, d_), const2),
        pl.BlockSpec((PQ_DIM, d_), const2),
        pl.BlockSpec((PEER_HEADS * N_KEYS, PQ_DIM // 2), const2),
        pl.BlockSpec((N_KEYS, D_KEY // 2), const2),
    ]
    args += [w["g"], w["wqt"], w["skx0"], w["sk1"]]
    feat = lambda rows: pl.BlockSpec((rows, tm), lambda i: (0, i))
    out_specs = [feat(d_), feat(PEER_HEADS * N_KEYS), feat(PEER_HEADS * N_KEYS), feat(PEER_HEADS)]
    out_shape = [
        jax.ShapeDtypeStruct((d_, n_), BF16),
        jax.ShapeDtypeStruct((PEER_HEADS * N_KEYS, n_), F32),
        jax.ShapeDtypeStruct((PEER_HEADS * N_KEYS, n_), BF16),
        jax.ShapeDtypeStruct((PEER_HEADS, n_), F32),
    ]
    if with_attn:
        out_specs = [tok] + out_specs
        out_shape = [jax.ShapeDtypeStruct((n_, d_), F32)] + out_shape
    outs = pl.pallas_call(
        functools.partial(_peer_front_kernel, tm=tm, with_attn=with_attn),
        grid=(n_ // tm,),
        in_specs=in_specs,
        out_specs=out_specs,
        out_shape=out_shape,
        scratch_shapes=[pltpu.VMEM((2, tm // V7X_LANES, PEER_HEADS * N_KEYS, V7X_LANES), F32)],
        compiler_params=_cparams("arbitrary"),
        name="peer_front_attn" if with_attn else "peer_front",
    )(*args)
    if with_attn:
        return outs[0], outs[1:]
    return x, outs


def _peer_weight_tile(lt, keys, ht_ref, pt_ref, t0_ref, t1_ref, th_ref):
    rb = 2 * V7X_SUBLANES
    lanes = slice(lt * V7X_LANES, (lt + 1) * V7X_LANES)
    bcast = lambda row: jnp.broadcast_to(row, (rb, V7X_LANES)).astype(BF16)
    th = [bcast(th_ref[h:h + 1, lanes]) for h in range(PEER_HEADS)]
    zero = jnp.zeros((rb, V7X_LANES), BF16)
    for al in keys:
        ta = [bcast(t0_ref[al * PEER_HEADS + h:al * PEER_HEADS + h + 1, lanes]) for h in range(PEER_HEADS)]
        for bg in range(N_KEYS // rb):
            w = None
            for h in range(PEER_HEADS):
                prod = ta[h] * t1_ref[h * N_KEYS + bg * rb:h * N_KEYS + (bg + 1) * rb, lanes]
                wh = jnp.where(prod >= th[h], prod, zero)
                w = wh if w is None else w + wh
            r0 = al * N_KEYS + bg * rb
            hv = ht_ref[r0:r0 + rb, lanes]
            gl = hv * (1.0 + lax.erf(hv * (1.0 / math.sqrt(2.0))))
            pt_ref[r0:r0 + rb, lanes] = gl.astype(BF16) * w


def _peer_half_step(xnt_ref, u_ref, u_rows, vt_ref, vt_cols, acc_ref, ht_w, pt_r, ht_r, pt_w,
                    t0_ref, t1_ref, th_ref, *, eb, tm):
    mxu_w = 2 * V7X_LANES
    pieces = []
    for c in range(tm // mxu_w):
        cols = slice(c * mxu_w, (c + 1) * mxu_w)

        def scores(cols=cols):
            ht_w[:, cols] = jnp.dot(u_ref[u_rows, :], xnt_ref[:, cols], preferred_element_type=F32)

        def values(cols=cols):
            acc_ref[:, cols] += jnp.dot(vt_ref[:, vt_cols], pt_r[:, cols], preferred_element_type=F32)

        pieces += [scores, values]
    for lt in range(tm // V7X_LANES):
        _peer_weight_tile(lt, range(eb // N_KEYS), ht_r, pt_w, t0_ref, t1_ref, th_ref)
        pieces[lt]()


def _peer_dense_kernel(xnt_ref, u_ref, vt_ref, t0a_ref, t1a_ref, tha_ref, t0b_ref, t1b_ref, thb_ref,
                       x1_ref, gf_ref, out_ref, acc_ref, ht0_ref, ht1_ref, pt0_ref, pt1_ref,
                       *, eb, ne, final_norm):
    g = pl.program_id(0)
    tm = acc_ref.shape[1]
    fv = 2 * g - 2

    @pl.when(g == 0)
    def _():
        ht1_ref[...] = jnp.zeros_like(ht1_ref)
        pt0_ref[...] = jnp.zeros_like(pt0_ref)

    @pl.when((g == 0) | (lax.rem(fv, ne) == 0))
    def _():
        acc_ref[...] = jnp.zeros_like(acc_ref)

    _peer_half_step(xnt_ref, u_ref, slice(0, eb), vt_ref, slice(0, eb), acc_ref,
                    ht0_ref, pt0_ref, ht1_ref, pt1_ref, t0a_ref, t1a_ref, tha_ref, eb=eb, tm=tm)
    _peer_half_step(xnt_ref, u_ref, slice(eb, 2 * eb), vt_ref, slice(eb, 2 * eb), acc_ref,
                    ht1_ref, pt1_ref, ht0_ref, pt0_ref, t0b_ref, t1b_ref, thb_ref, eb=eb, tm=tm)

    @pl.when((g > 0) & (lax.rem(fv + 1, ne) == ne - 1))
    def _():
        y = x1_ref[...] + acc_ref[...].T
        if final_norm:
            y = _rms(y, gf_ref[...])
        out_ref[...] = y


def _peer_dense(x1, front, w, gf, *, layer, tm, eb, final_norm):
    xnt, t0, t1, th = front
    n_, d_ = x1.shape
    ne = N_EXPERTS // eb
    assert n_ % tm == 0 and N_EXPERTS % eb == 0 and eb % N_KEYS == 0 and ne % 2 == 0
    assert tm % (2 * V7X_LANES) == 0
    nblocks = (n_ // tm) * ne
    last = nblocks - 1
    tok_mm1 = lambda g: jnp.minimum(2 * g, last) // ne
    tok_a = lambda g: jnp.clip(2 * g - 1, 0, last) // ne
    tok_v = lambda g: jnp.maximum(2 * g - 2, 0) // ne
    blk_a = lambda g: jnp.clip(2 * g - 1, 0, last) % ne
    blk_b = lambda g: jnp.minimum(2 * g, last) % ne
    pair_mm1 = lambda g: (jnp.minimum(2 * g, last) % ne) // 2
    pair_v = lambda g: (jnp.maximum(2 * g - 2, 0) % ne) // 2
    feat = lambda rows, tok: pl.BlockSpec((rows, tm), lambda g: (0, tok(g)))
    nk = PEER_HEADS * N_KEYS
    t0_rows = (eb // N_KEYS) * PEER_HEADS
    return pl.pallas_call(
        functools.partial(_peer_dense_kernel, eb=eb, ne=ne, final_norm=final_norm),
        grid=(nblocks // 2 + 1,),
        in_specs=[
            feat(d_, tok_mm1),
            pl.BlockSpec((None, 2 * eb, d_), lambda g: (layer, pair_mm1(g), 0)),
            pl.BlockSpec((None, d_, 2 * eb), lambda g: (layer, 0, pair_v(g))),
            pl.BlockSpec((t0_rows, tm), lambda g: (blk_a(g), tok_a(g))), feat(nk, tok_a), feat(PEER_HEADS, tok_a),
            pl.BlockSpec((t0_rows, tm), lambda g: (blk_b(g), tok_mm1(g))), feat(nk, tok_mm1),
            feat(PEER_HEADS, tok_mm1),
            pl.BlockSpec((tm, d_), lambda g: (tok_v(g), 0)),
            pl.BlockSpec((1, d_), lambda g: (0, 0)),
        ],
        out_specs=pl.BlockSpec((tm, d_), lambda g: (tok_v(g), 0)),
        out_shape=jax.ShapeDtypeStruct((n_, d_), F32),
        scratch_shapes=[
            pltpu.VMEM((d_, tm), F32),
            pltpu.VMEM((eb, tm), F32), pltpu.VMEM((eb, tm), F32),
            pltpu.VMEM((eb, tm), BF16), pltpu.VMEM((eb, tm), BF16),
        ],
        compiler_params=_cparams("arbitrary"),
        name="peer_dense",
    )(xnt, w["u"], w["vt"], t0, t1, th, t0, t1, th, x1, gf)


def _rope(x, cos, sin_signed):
    half = HEAD_DIM // 2
    lane = lax.broadcasted_iota(jnp.int32, (1, V7X_LANES), 1)
    first = (lane % HEAD_DIM) < half
    outs = []
    for t in range(x.shape[1] // V7X_LANES):
        xt = x[:, t * V7X_LANES:(t + 1) * V7X_LANES]
        swapped = jnp.where(first, pltpu.roll(xt, V7X_LANES - half, axis=1), pltpu.roll(xt, half, axis=1))
        outs.append(xt * cos + swapped * sin_signed)
    return jnp.concatenate(outs, axis=1)


def _qkv_kernel(x_ref, gkv_ref, wkv_ref, gq_ref, wq_ref, cos_ref, sin_ref, q_ref, k_ref, v_ref):
    x = x_ref[...]
    cos = cos_ref[...]
    sin = sin_ref[...]
    kv = jnp.dot(_rms(x, gkv_ref[...]).astype(BF16), wkv_ref[...], preferred_element_type=F32)
    k_ref[...] = _rope(kv[:, :KV_DIM], cos, sin)
    v_ref[...] = kv[:, KV_DIM:]
    q = jnp.dot(_rms(x, gq_ref[...]).astype(BF16), wq_ref[...], preferred_element_type=F32)
    q_ref[...] = (_rope(q, cos, sin) * (HEAD_DIM ** -0.5)).astype(q_ref.dtype)


def _qkv(x, w, cos, sin, *, tm, table_blocks, q_dtype):
    n_, d_ = x.shape
    assert n_ % tm == 0
    const2 = lambda i: (0, 0)
    tok = lambda cols: pl.BlockSpec((tm, cols), lambda i: (i, 0))
    tab = pl.BlockSpec((tm, V7X_LANES), lambda i: (i % table_blocks, 0))
    return pl.pallas_call(
        _qkv_kernel,
        grid=(n_ // tm,),
        in_specs=[tok(d_), pl.BlockSpec((1, d_), const2), pl.BlockSpec((d_, 2 * KV_DIM), const2),
                  pl.BlockSpec((1, d_), const2), pl.BlockSpec((d_, Q_DIM), const2), tab, tab],
        out_specs=[tok(Q_DIM), tok(KV_DIM), tok(KV_DIM)],
        out_shape=[jax.ShapeDtypeStruct((n_, Q_DIM), q_dtype),
                   jax.ShapeDtypeStruct((n_, KV_DIM), F32),
                   jax.ShapeDtypeStruct((n_, KV_DIM), F32)],
        compiler_params=_cparams("arbitrary"),
        name="qkv_rope",
    )(x, w["g_kv"], w["w_kv"], w["g_q"], w["w_q"], cos, sin)


def _sink_attend(s, mask, sink, v):
    s = jnp.where(mask, s, -jnp.inf)
    m = jnp.maximum(jnp.max(s, axis=-1, keepdims=True), sink)
    p = jnp.exp(s - m)
    denom = jnp.sum(p, axis=-1, keepdims=True) + jnp.exp(sink - m)
    return jnp.dot(p.astype(BF16), v, preferred_element_type=F32) / denom


def _swa_prompt_kernel(sink_ref, q_ref, kp_ref, kc_ref, vp_ref, vc_ref, o_ref):
    n = pl.program_id(1)
    c = WINDOW
    k2 = jnp.concatenate([kp_ref[...], kc_ref[...]], axis=0).astype(BF16)
    v2 = jnp.concatenate([vp_ref[...], vc_ref[...]], axis=0).astype(BF16)
    qi = lax.broadcasted_iota(jnp.int32, (c, 2 * c), 0)
    kj = lax.broadcasted_iota(jnp.int32, (c, 2 * c), 1)
    mask = (kj > qi) & (kj <= qi + c) & ((kj >= c) | (n > 0))
    outs = []
    for hd in range(N_HEADS):
        kvh = hd // GROUP
        q = q_ref[:, hd * HEAD_DIM:(hd + 1) * HEAD_DIM]
        kk = k2[:, kvh * HEAD_DIM:(kvh + 1) * HEAD_DIM]
        s = lax.dot_general(q, kk, (((1,), (1,)), ((), ())), preferred_element_type=F32)
        outs.append(_sink_attend(s, mask, sink_ref[hd], v2[:, kvh * HEAD_DIM:(kvh + 1) * HEAD_DIM]))
    o_ref[...] = jnp.concatenate(outs, axis=1).astype(BF16)


def _swa_prompt(q, k, v, sinks, *, batch, seq):
    nb = seq // WINDOW
    cur = lambda cols: pl.BlockSpec((WINDOW, cols), lambda b, n: (b * nb + n, 0))
    prev = lambda cols: pl.BlockSpec((WINDOW, cols), lambda b, n: (b * nb + jnp.maximum(n - 1, 0), 0))
    return pl.pallas_call(
        _swa_prompt_kernel,
        grid=(batch, nb),
        in_specs=[pl.BlockSpec(memory_space=pltpu.SMEM),
                  cur(Q_DIM), prev(KV_DIM), cur(KV_DIM), prev(KV_DIM), cur(KV_DIM)],
        out_specs=cur(Q_DIM),
        out_shape=jax.ShapeDtypeStruct((batch * seq, Q_DIM), BF16),
        compiler_params=_cparams("arbitrary", "arbitrary"),
        name="swa_prompt",
    )(sinks, q, k, k, v, v)


def _swa_sample_kernel(sink_ref, q_ref, kc_ref, kn_ref, vc_ref, vn_ref, o_ref, ko_ref, vo_ref, *, nq, nb):
    c = WINDOW
    rows = GROUP * nq
    qi = lax.broadcasted_iota(jnp.int32, (rows, 2 * c), 0) % nq
    kj = lax.broadcasted_iota(jnp.int32, (rows, 2 * c), 1)
    mask = ((kj < c) & (kj > qi)) | ((kj >= c) & (kj - c <= qi))
    mask = mask[None]
    fill = jnp.zeros((nb, c - nq, KV_DIM), F32)
    k2 = jnp.concatenate([kc_ref[...], kn_ref[...], fill], axis=1).astype(BF16)
    v2 = jnp.concatenate([vc_ref[...], vn_ref[...], fill], axis=1).astype(BF16)
    q = q_ref[...]
    outs = [None] * N_HEADS
    for kvh in range(N_KV_HEADS):
        heads = range(kvh * GROUP, (kvh + 1) * GROUP)
        cols = slice(kvh * HEAD_DIM, (kvh + 1) * HEAD_DIM)
        qs = jnp.concatenate([q[:, :, hd * HEAD_DIM:(hd + 1) * HEAD_DIM] for hd in heads], axis=1).astype(BF16)
        sink = jnp.concatenate([jnp.full((1, nq, 1), sink_ref[hd], F32) for hd in heads], axis=1)
        s = jnp.einsum("bqd,bkd->bqk", qs, k2[:, :, cols], preferred_element_type=F32)
        s = jnp.where(mask, s, -jnp.inf)
        m = jnp.maximum(jnp.max(s, axis=-1, keepdims=True), sink)
        p = jnp.exp(s - m)
        denom = jnp.sum(p, axis=-1, keepdims=True) + jnp.exp(sink - m)
        o = jnp.einsum("bqk,bkd->bqd", p.astype(BF16), v2[:, :, cols], preferred_element_type=F32) / denom
        for g, hd in enumerate(heads):
            outs[hd] = o[:, g * nq:(g + 1) * nq, :]
    o_ref[...] = jnp.concatenate(outs, axis=2)
    for src_c, src_n, dst in ((kc_ref, kn_ref, ko_ref), (vc_ref, vn_ref, vo_ref)):
        dst[:, 0:c - nq, :] = src_c[:, nq:c, :]
        dst[:, c - nq:c, :] = src_n[...]


def _swa_sample(q, kc, kn, vc, vn, sinks):
    b_, nq, _ = q.shape
    assert nq <= WINDOW
    nb = _pick(b_, (8, 4, 2, 1))
    blk = lambda rows, cols: pl.BlockSpec((nb, rows, cols), lambda b: (b, 0, 0))
    return pl.pallas_call(
        functools.partial(_swa_sample_kernel, nq=nq, nb=nb),
        grid=(b_ // nb,),
        in_specs=[pl.BlockSpec(memory_space=pltpu.SMEM),
                  blk(nq, Q_DIM), blk(WINDOW, KV_DIM), blk(nq, KV_DIM), blk(WINDOW, KV_DIM), blk(nq, KV_DIM)],
        out_specs=[blk(nq, Q_DIM), blk(WINDOW, KV_DIM), blk(WINDOW, KV_DIM)],
        out_shape=[jax.ShapeDtypeStruct((b_, nq, Q_DIM), F32),
                   jax.ShapeDtypeStruct((b_, WINDOW, KV_DIM), F32),
                   jax.ShapeDtypeStruct((b_, WINDOW, KV_DIM), F32)],
        compiler_params=_cparams("arbitrary"),
        name="swa_sample",
    )(sinks, q, kc, kn, vc, vn)


def _rope_tables(pos):
    half = HEAD_DIM // 2
    inv = ROPE_THETA ** (-jnp.arange(half, dtype=F32) / half)
    ang = pos.astype(F32)[:, None] * inv[None, :]
    cos = jnp.cos(ang)
    sin = jnp.sin(ang)
    reps = V7X_LANES // HEAD_DIM
    cos_t = jnp.tile(jnp.concatenate([cos, cos], axis=1), (1, reps))
    sin_t = jnp.tile(jnp.concatenate([-sin, sin], axis=1), (1, reps))
    return cos_t, sin_t


def _pick(n, cands):
    for c in cands:
        if n % c == 0:
            return c
    raise ValueError(f"no tile for {n}")


def _peer(x1, pw, gf, attn=None, *, layer, final_norm):
    n_ = x1.shape[0]
    x1, front = _peer_front(x1, pw, attn, tm=_pick(n_, (256, 128)))
    return _peer_dense(x1, front, pw, gf, layer=layer, tm=_pick(n_, (512, 256, 128)), eb=1024,
                       final_norm=final_norm)


def kernel(x_prompt, x_sample, state_conv, state_h, cache_k, cache_v, rg_norm, rg_w_in, rg_conv_w,
           rg_conv_b, rg_wa, rg_ba, rg_wx, rg_bx, rg_lambda, rg_w_out, kv_norm, w_kv, attn_norm, w_q,
           sinks, w_o, ffn_norm, peer_wq, peer_subkeys, peer_u, peer_v, final_norm):
    d = D_MODEL
    row = lambda a: a.reshape(1, -1).astype(F32)
    rgw = dict(g=row(rg_norm[0]), w_in=rg_w_in[0].astype(BF16), conv_w=rg_conv_w[0],
               conv_b=row(rg_conv_b[0]), wa=rg_wa[0].astype(BF16), ba=row(rg_ba[0]),
               wx=rg_wx[0].astype(BF16), bx=row(rg_bx[0]), lam=row(rg_lambda[0]),
               w_out=rg_w_out[0].astype(BF16))
    u_all = peer_u.astype(BF16)
    vt_all = jnp.swapaxes(peer_v.astype(BF16), 1, 2)
    peer_w = []
    for l in range(2):
        wq = peer_wq[l].reshape(d, PEER_HEADS, 2, D_KEY // 2).transpose(2, 1, 3, 0).reshape(PQ_DIM, d)
        skx0 = jnp.einsum("nc,hg->nhgc", peer_subkeys[l, 0], jnp.eye(PEER_HEADS, dtype=F32))
        skx0 = skx0.reshape(PEER_HEADS * N_KEYS, PQ_DIM // 2)
        peer_w.append(dict(g=row(ffn_norm[l]), wqt=wq.astype(BF16), skx0=skx0.astype(BF16),
                           sk1=peer_subkeys[l, 1].astype(BF16), u=u_all, vt=vt_all))
    attw = dict(g_kv=row(kv_norm), w_kv=w_kv.astype(BF16), g_q=row(attn_norm[0]), w_q=w_q[0].astype(BF16))
    wo = w_o[0].astype(BF16)
    sink = sinks[0].astype(F32)
    gf = row(final_norm)

    bp, tp, _ = x_prompt.shape
    steps_p = _pick(tp, (256, 128, 64, 32, 16, 8))
    pad_p = V7X_SUBLANES
    x1, conv_p, h_p = _rg_layer(x_prompt, jnp.zeros((bp, pad_p, d), F32), jnp.zeros((bp, 1, d), F32),
                                rgw, stride=1, steps=steps_p)
    xp = _peer(x1.reshape(bp * tp, d), peer_w[0], gf, layer=0, final_norm=False)
    cos_p, sin_p = _rope_tables(jnp.arange(tp))
    tm_p = _pick(tp, (512, 256, 128))
    q, k, v = _qkv(xp, attw, cos_p, sin_p, tm=tm_p, table_blocks=tp // tm_p, q_dtype=BF16)
    o = _swa_prompt(q, k, v, sink, batch=bp, seq=tp)
    y_prompt = _peer(xp, peer_w[1], gf, attn=(o, wo), layer=1, final_norm=True).reshape(bp, tp, d)
    k_p = k.reshape(bp, tp, N_KV_HEADS, HEAD_DIM)[:, -WINDOW:]
    v_p = v.reshape(bp, tp, N_KV_HEADS, HEAD_DIM)[:, -WINDOW:]
    conv_p = conv_p[None]
    h_p = h_p.reshape(1, bp, d)

    bs, ts, _ = x_sample.shape
    xs = x_sample.transpose(1, 0, 2).reshape(1, ts * bs, d)
    conv0 = state_conv[0].transpose(1, 0, 2).reshape(1, (CONV_W - 1) * bs, d)
    x1s, conv_s, h_s = _rg_layer(xs, conv0, state_h[0][None], rgw, stride=bs, steps=ts)
    xs1 = _peer(x1s.reshape(ts * bs, d), peer_w[0], gf, layer=0, final_norm=False)
    xs1 = xs1.reshape(ts, bs, d).transpose(1, 0, 2).reshape(bs * ts, d)
    cos_s, sin_s = _rope_tables(PAST_LEN + jnp.arange(ts))
    cos_s = jnp.tile(cos_s, (bs, 1))
    sin_s = jnp.tile(sin_s, (bs, 1))
    tm_s = _pick(bs * ts, (512, 256, 128, 64, 32, 16, 8))
    qs, ks, vs = _qkv(xs1, attw, cos_s, sin_s, tm=tm_s, table_blocks=(bs * ts) // tm_s, q_dtype=F32)
    ck = cache_k.reshape(bs, WINDOW, KV_DIM)
    cv = cache_v.reshape(bs, WINDOW, KV_DIM)
    kn = ks.reshape(bs, ts, KV_DIM)
    vn = vs.reshape(bs, ts, KV_DIM)
    os_, k_s, v_s = _swa_sample(qs.reshape(bs, ts, Q_DIM), ck, kn, cv, vn, sink)
    y_sample = _peer(xs1, peer_w[1], gf, attn=(os_.reshape(bs * ts, Q_DIM), wo), layer=1,
                     final_norm=True).reshape(bs, ts, d)
    k_s = k_s.reshape(bs, WINDOW, N_KV_HEADS, HEAD_DIM)
    v_s = v_s.reshape(bs, WINDOW, N_KV_HEADS, HEAD_DIM)
    conv_s = conv_s.reshape(CONV_W - 1, bs, d).transpose(1, 0, 2)[None]
    h_s = h_s.reshape(1, bs, d)

    return (y_prompt, y_sample, conv_p, h_p, k_p, v_p, conv_s, h_s, k_s, v_s)
```

```python
import functools
import math

import jax
import jax.numpy as jnp
from jax import lax
from jax.experimental import pallas as pl
from jax.experimental.pallas import tpu as pltpu

D_MODEL = 1024
PAST_LEN = 16384
D_RNN = D_MODEL
RG_BLOCKS = 8
RG_BLOCK_W = D_RNN // RG_BLOCKS
CONV_W = 4
RG_C = 8.0
HEAD_DIM = 64
N_HEADS = D_MODEL // HEAD_DIM
N_KV_HEADS = 4
GROUP = N_HEADS // N_KV_HEADS
WINDOW = 128
ROPE_THETA = 10000.0
N_KEYS = 128
N_EXPERTS = N_KEYS * N_KEYS
PEER_HEADS = 8
PEER_TOPK = 16
D_KEY = 256
EPS = 1e-6

KV_DIM = N_KV_HEADS * HEAD_DIM
Q_DIM = N_HEADS * HEAD_DIM
PQ_DIM = PEER_HEADS * D_KEY

V7X_LANES = 128
V7X_SUBLANES = 8
V7X_VMEM_LIMIT_BYTES = 56 * 1024 * 1024

BF16 = jnp.bfloat16
F32 = jnp.float32


def _cparams(*sem, flags=None):
    return pltpu.CompilerParams(dimension_semantics=sem, vmem_limit_bytes=V7X_VMEM_LIMIT_BYTES, flags=flags)


def _rms(x, g):
    return x * lax.rsqrt(jnp.mean(x * x, axis=-1, keepdims=True) + EPS) * g


def _gelu_tanh(x):
    u2 = (-2.0 * math.sqrt(2.0 / math.pi)) * (x * (1.0 + 0.044715 * (x * x)))
    return x / (1.0 + jnp.exp(u2))


def _sigmoid(x):
    return 1.0 / (1.0 + jnp.exp(-x))


def _rg_kernel(x_ref, conv0_ref, h0_ref, g_ref, win_ref, cw_ref, cb_ref, wa_ref, ba_ref,
               wx_ref, bx_ref, lam_ref, wout_ref,
               y_ref, conv_ref, hlast_ref,
               xbuf_ref, hcar_ref, *, stride, steps, pad):
    rows = stride * steps
    tail = (CONV_W - 1) * stride
    c = pl.program_id(1)

    @pl.when(c == 0)
    def _():
        xbuf_ref[0:pad, :] = conv0_ref[0]
        hcar_ref[...] = h0_ref[0]

    x = x_ref[0]
    xn = _rms(x, g_ref[...]).astype(BF16)
    proj = jnp.dot(xn, win_ref[...], preferred_element_type=F32)
    gate = _gelu_tanh(proj[:, :D_RNN])
    xr = proj[:, D_RNN:]
    xbuf_ref[pad:pad + rows, :] = xr

    y = cb_ref[...] + cw_ref[0:1, :] * xbuf_ref[pad - 3 * stride:pad - 3 * stride + rows, :]
    y = y + cw_ref[1:2, :] * xbuf_ref[pad - 2 * stride:pad - 2 * stride + rows, :]
    y = y + cw_ref[2:3, :] * xbuf_ref[pad - stride:pad - stride + rows, :]
    xc = y + cw_ref[3:4, :] * xr

    conv_ref[0] = xbuf_ref[pad + rows - tail:pad + rows, :]
    xbuf_ref[0:pad, :] = xbuf_ref[rows:rows + pad, :]

    xcb = xc.astype(BF16)
    rs, is_ = [], []
    for n in range(RG_BLOCKS):
        blk = xcb[:, n * RG_BLOCK_W:(n + 1) * RG_BLOCK_W]
        rs.append(jnp.dot(blk, wa_ref[n], preferred_element_type=F32))
        is_.append(jnp.dot(blk, wx_ref[n], preferred_element_type=F32))
    r = _sigmoid(jnp.concatenate(rs, axis=1) + ba_ref[...])
    i = _sigmoid(jnp.concatenate(is_, axis=1) + bx_ref[...])

    nlam = -lam_ref[...]
    softplus = jnp.maximum(nlam, 0.0) + jnp.log1p(jnp.exp(-jnp.abs(nlam)))
    log_a = -RG_C * r * softplus
    a = jnp.exp(log_a)
    mult = jnp.sqrt(-jnp.tanh(log_a) * (a * a + 1.0))
    b = mult * i * xc

    group = V7X_SUBLANES if (stride == 1 and steps % V7X_SUBLANES == 0) else steps
    row = lax.broadcasted_iota(jnp.int32, (rows, D_RNN), 0) % (group * stride)
    s = 1
    while s < group:
        sh = s * stride
        a_sh = pltpu.roll(a, sh, axis=0)
        b_sh = pltpu.roll(b, sh, axis=0)
        m = row >= sh
        b = jnp.where(m, a * b_sh + b, b)
        a = jnp.where(m, a * a_sh, a)
        s *= 2
    grows = group * stride
    carry = hcar_ref[...]
    hs = []
    for g0 in range(0, rows, grows):
        cg = carry if stride == 1 else jnp.concatenate([carry] * group, axis=0)
        hg_ = a[g0:g0 + grows, :] * cg + b[g0:g0 + grows, :]
        hs.append(hg_)
        carry = hg_[grows - stride:grows, :]
    h = hs[0] if len(hs) == 1 else jnp.concatenate(hs, axis=0)
    hl = carry
    hcar_ref[...] = hl
    hlast_ref[0] = hl

    hg = (h * gate).astype(BF16)
    y_ref[0] = x + jnp.dot(hg, wout_ref[...], preferred_element_type=F32)


def _rg_layer(x3, conv0, h0, w, *, stride, steps):
    g_, r_, d_ = x3.shape
    rows = stride * steps
    assert r_ % rows == 0
    pad = conv0.shape[1]
    tail = (CONV_W - 1) * stride
    nchunks = r_ // rows
    const2 = lambda g, c: (0, 0)
    const3 = lambda g, c: (0, 0, 0)
    kern = functools.partial(_rg_kernel, stride=stride, steps=steps, pad=pad)
    return pl.pallas_call(
        kern,
        grid=(g_, nchunks),
        in_specs=[
            pl.BlockSpec((1, rows, d_), lambda g, c: (g, c, 0)),
            pl.BlockSpec((1, pad, d_), lambda g, c: (g, 0, 0)),
            pl.BlockSpec((1, stride, d_), lambda g, c: (g, 0, 0)),
            pl.BlockSpec((1, d_), const2),
            pl.BlockSpec((d_, 2 * D_RNN), const2),
            pl.BlockSpec((CONV_W, D_RNN), const2),
            pl.BlockSpec((1, D_RNN), const2),
            pl.BlockSpec((RG_BLOCKS, RG_BLOCK_W, RG_BLOCK_W), const3),
            pl.BlockSpec((1, D_RNN), const2),
            pl.BlockSpec((RG_BLOCKS, RG_BLOCK_W, RG_BLOCK_W), const3),
            pl.BlockSpec((1, D_RNN), const2),
            pl.BlockSpec((1, D_RNN), const2),
            pl.BlockSpec((D_RNN, d_), const2),
        ],
        out_specs=[
            pl.BlockSpec((1, rows, d_), lambda g, c: (g, c, 0)),
            pl.BlockSpec((1, tail, D_RNN), lambda g, c: (g, 0, 0)),
            pl.BlockSpec((1, stride, D_RNN), lambda g, c: (g, 0, 0)),
        ],
        out_shape=[
            jax.ShapeDtypeStruct((g_, r_, d_), F32),
            jax.ShapeDtypeStruct((g_, tail, D_RNN), F32),
            jax.ShapeDtypeStruct((g_, stride, D_RNN), F32),
        ],
        scratch_shapes=[
            pltpu.VMEM((pad + rows, D_RNN), F32),
            pltpu.VMEM((stride, D_RNN), F32),
        ],
        compiler_params=_cparams("arbitrary", "arbitrary"),
        name="rg_layer",
    )(x3, conv0, h0, w["g"], w["w_in"], w["conv_w"], w["conv_b"], w["wa"], w["ba"],
      w["wx"], w["bx"], w["lam"], w["w_out"])


def _ce(v, i, j):
    hi = jnp.maximum(v[i], v[j])
    lo = jnp.minimum(v[i], v[j])
    v[i], v[j] = hi, lo


def _sort16(v):
    n = 16
    k = 2
    while k <= n:
        j = k // 2
        while j >= 1:
            for i in range(n):
                l = i ^ j
                if l > i:
                    if (i & k) == 0:
                        _ce(v, i, l)
                    else:
                        _ce(v, l, i)
            j //= 2
        k *= 2


def _merge_top16(a, b):
    v = [jnp.maximum(a[k], b[15 - k]) for k in range(16)]
    j = 8
    while j >= 1:
        for i in range(16):
            l = i ^ j
            if l > i:
                _ce(v, i, l)
        j //= 2
    return v


def _top16(vs):
    groups = []
    for g in range(len(vs) // 16):
        lst = list(vs[16 * g:16 * g + 16])
        _sort16(lst)
        groups.append(lst)
    while len(groups) > 1:
        groups = [_merge_top16(groups[i], groups[i + 1]) for i in range(0, len(groups), 2)]
    return groups[0]


def _kth_product(sv0, sv1, mul):
    cands = [mul(sv0[0], sv1[j]) for j in range(PEER_TOPK)]
    for i in range(1, PEER_TOPK):
        for j in range(PEER_TOPK // (i + 1)):
            cands.append(mul(sv0[i], sv1[j]))
    padded = list(cands)
    while len(padded) % 16:
        padded.append(jnp.full_like(cands[0], -1.0))
    return _top16(padded)[PEER_TOPK - 1], cands


def _peer_select(sc_ref, lt):
    sv0 = _top16([sc_ref[0, lt, n * PEER_HEADS:(n + 1) * PEER_HEADS, :] for n in range(N_KEYS)])
    sv1 = _top16([sc_ref[1, lt, pl.ds(n, PEER_HEADS, stride=N_KEYS), :] for n in range(N_KEYS)])
    theta, cands = _kth_product(sv0, sv1, lambda a, b: a * b)
    z = jnp.zeros_like(theta)
    for cnd in cands:
        z = z + jnp.where(cnd >= theta, cnd, 0.0)
    scale = 0.5 / z
    sv0n = [(v * scale).astype(BF16) for v in sv0]
    sv1b = [v.astype(BF16) for v in sv1]
    theta_n, _ = _kth_product(sv0n, sv1b, lambda a, b: (a * b).astype(F32))
    return scale, theta_n


def _peer_front_kernel(*refs, tm, with_attn):
    if with_attn:
        (x_ref, o_ref, wo_ref, g_ref, wqt_ref, skx_ref, sk_ref,
         x1_ref, xnt_ref, t0_ref, t1_ref, th_ref, sc_ref) = refs
        x1 = x_ref[...] + jnp.dot(o_ref[...].astype(BF16), wo_ref[...], preferred_element_type=F32)
        x1_ref[...] = x1
    else:
        (x_ref, g_ref, wqt_ref, skx_ref, sk_ref, xnt_ref, t0_ref, t1_ref, th_ref, sc_ref) = refs
        x1 = x_ref[...]
    nh = PEER_HEADS
    xn = _rms(x1, g_ref[...])
    xnt = xn.T.astype(BF16)
    xnt_ref[...] = xnt
    qt = jnp.dot(wqt_ref[...], xnt, preferred_element_type=F32).astype(BF16)
    s0 = jnp.dot(skx_ref[...], qt[0:PQ_DIM // 2, :], preferred_element_type=F32)
    mx = s0[0:nh, :]
    for n in range(1, N_KEYS):
        mx = jnp.maximum(mx, s0[n * nh:(n + 1) * nh, :])
    for n in range(N_KEYS):
        t = jnp.exp(s0[n * nh:(n + 1) * nh, :] - mx)
        for lt in range(tm // V7X_LANES):
            sc_ref[0, lt, n * nh:(n + 1) * nh, :] = t[:, lt * V7X_LANES:(lt + 1) * V7X_LANES]
    for h in range(nh):
        base = PQ_DIM // 2 + h * (D_KEY // 2)
        s = jnp.dot(sk_ref[...], qt[base:base + D_KEY // 2, :], preferred_element_type=F32)
        t = jnp.exp(s - jnp.max(s, axis=0, keepdims=True)).astype(BF16)
        t1_ref[h * N_KEYS:(h + 1) * N_KEYS, :] = t
        t = t.astype(F32)
        for lt in range(tm // V7X_LANES):
            sc_ref[1, lt, h * N_KEYS:(h + 1) * N_KEYS, :] = t[:, lt * V7X_LANES:(lt + 1) * V7X_LANES]
    for lt in range(tm // V7X_LANES):
        lanes = pl.ds(lt * V7X_LANES, V7X_LANES)
        scale, theta = _peer_select(sc_ref, lt)
        th_ref[:, lanes] = theta
        for n in range(N_KEYS):
            t0_ref[n * nh:(n + 1) * nh, lanes] = (
                sc_ref[0, lt, n * nh:(n + 1) * nh, :] * scale).astype(BF16).astype(F32)


def _peer_front(x, w, attn=None, *, tm):
    n_, d_ = x.shape
    assert n_ % tm == 0
    with_attn = attn is not None
    const2 = lambda i: (0, 0)
    tok = pl.BlockSpec((tm, d_), lambda i: (i, 0))
    in_specs = [tok]
    args = [x]
    if with_attn:
        in_specs += [pl.BlockSpec((tm, Q_DIM), lambda i: (i, 0)), pl.BlockSpec((Q_DIM, d_), const2)]
        args += list(attn)
    in_specs += [
        pl.BlockSpec((1, d_), const2),
        pl.BlockSpec((PQ_DIM, d_), const2),
        pl.BlockSpec((PEER_HEADS * N_KEYS, PQ_DIM // 2), const2),
        pl.BlockSpec((N_KEYS, D_KEY // 2), const2),
    ]
    args += [w["g"], w["wqt"], w["skx0"], w["sk1"]]
    feat = lambda rows: pl.BlockSpec((rows, tm), lambda i: (0, i))
    out_specs = [feat(d_), feat(PEER_HEADS * N_KEYS), feat(PEER_HEADS * N_KEYS), feat(PEER_HEADS)]
    out_shape = [
        jax.ShapeDtypeStruct((d_, n_), BF16),
        jax.ShapeDtypeStruct((PEER_HEADS * N_KEYS, n_), F32),
        jax.ShapeDtypeStruct((PEER_HEADS * N_KEYS, n_), BF16),
        jax.ShapeDtypeStruct((PEER_HEADS, n_), F32),
    ]
    if with_attn:
        out_specs = [tok] + out_specs
        out_shape = [jax.ShapeDtypeStruct((n_, d_), F32)] + out_shape
    outs = pl.pallas_call(
        functools.partial(_peer_front_kernel, tm=tm, with_attn=with_attn),
        grid=(n_ // tm,),
        in_specs=in_specs,
        out_specs=out_specs,
        out_shape=out_shape,
        scratch_shapes=[pltpu.VMEM((2, tm // V7X_LANES, PEER_HEADS * N_KEYS, V7X_LANES), F32)],
        compiler_params=_cparams("arbitrary"),
        name="peer_front_attn" if with_attn else "peer_front",
    )(*args)
    if with_attn:
        return outs[0], outs[1:]
    return x, outs


def _peer_weight_tile(lt, keys, ht_ref, pt_ref, t0_ref, t1_ref, th_ref):
    rb = 2 * V7X_SUBLANES
    lanes = slice(lt * V7X_LANES, (lt + 1) * V7X_LANES)
    bcast = lambda row: jnp.broadcast_to(row, (rb, V7X_LANES)).astype(BF16)
    th = [bcast(th_ref[h:h + 1, lanes]) for h in range(PEER_HEADS)]
    zero = jnp.zeros((rb, V7X_LANES), BF16)
    for al in keys:
        ta = [bcast(t0_ref[al * PEER_HEADS + h:al * PEER_HEADS + h + 1, lanes]) for h in range(PEER_HEADS)]
        for bg in range(N_KEYS // rb):
            w = None
            for h in range(PEER_HEADS):
                prod = ta[h] * t1_ref[h * N_KEYS + bg * rb:h * N_KEYS + (bg + 1) * rb, lanes]
                wh = jnp.where(prod >= th[h], prod, zero)
                w = wh if w is None else w + wh
            r0 = al * N_KEYS + bg * rb
            hv = ht_ref[r0:r0 + rb, lanes]
            gl = hv * (1.0 + lax.erf(hv * (1.0 / math.sqrt(2.0))))
            pt_ref[r0:r0 + rb, lanes] = gl.astype(BF16) * w


def _peer_half_step(xnt_ref, u_ref, u_rows, vt_ref, vt_cols, acc_ref, ht_w, pt_r, ht_r, pt_w,
                    t0_ref, t1_ref, th_ref, *, eb, tm):
    mxu_w = 2 * V7X_LANES
    pieces = []
    for c in range(tm // mxu_w):
        cols = slice(c * mxu_w, (c + 1) * mxu_w)

        def scores(cols=cols):
            ht_w[:, cols] = jnp.dot(u_ref[u_rows, :], xnt_ref[:, cols], preferred_element_type=F32)

        def values(cols=cols):
            acc_ref[:, cols] += jnp.dot(vt_ref[:, vt_cols], pt_r[:, cols], preferred_element_type=F32)

        pieces += [scores, values]
    for lt in range(tm // V7X_LANES):
        _peer_weight_tile(lt, range(eb // N_KEYS), ht_r, pt_w, t0_ref, t1_ref, th_ref)
        pieces[lt]()


def _peer_dense_kernel(xnt_ref, u_ref, vt_ref, t0a_ref, t1a_ref, tha_ref, t0b_ref, t1b_ref, thb_ref,
                       x1_ref, gf_ref, out_ref, acc_ref, ht0_ref, ht1_ref, pt0_ref, pt1_ref,
                       *, eb, ne, final_norm):
    g = pl.program_id(0)
    tm = acc_ref.shape[1]
    fv = 2 * g - 2

    @pl.when(g == 0)
    def _():
        ht1_ref[...] = jnp.zeros_like(ht1_ref)
        pt0_ref[...] = jnp.zeros_like(pt0_ref)

    @pl.when((g == 0) | (lax.rem(fv, ne) == 0))
    def _():
        acc_ref[...] = jnp.zeros_like(acc_ref)

    _peer_half_step(xnt_ref, u_ref, slice(0, eb), vt_ref, slice(0, eb), acc_ref,
                    ht0_ref, pt0_ref, ht1_ref, pt1_ref, t0a_ref, t1a_ref, tha_ref, eb=eb, tm=tm)
    _peer_half_step(xnt_ref, u_ref, slice(eb, 2 * eb), vt_ref, slice(eb, 2 * eb), acc_ref,
                    ht1_ref, pt1_ref, ht0_ref, pt0_ref, t0b_ref, t1b_ref, thb_ref, eb=eb, tm=tm)

    @pl.when((g > 0) & (lax.rem(fv + 1, ne) == ne - 1))
    def _():
        y = x1_ref[...] + acc_ref[...].T
        if final_norm:
            y = _rms(y, gf_ref[...])
        out_ref[...] = y


def _peer_dense(x1, front, w, gf, *, layer, tm, eb, final_norm):
    xnt, t0, t1, th = front
    n_, d_ = x1.shape
    ne = N_EXPERTS // eb
    assert n_ % tm == 0 and N_EXPERTS % eb == 0 and eb % N_KEYS == 0 and ne % 2 == 0
    assert tm % (2 * V7X_LANES) == 0
    nblocks = (n_ // tm) * ne
    last = nblocks - 1
    tok_mm1 = lambda g: jnp.minimum(2 * g, last) // ne
    tok_a = lambda g: jnp.clip(2 * g - 1, 0, last) // ne
    tok_v = lambda g: jnp.maximum(2 * g - 2, 0) // ne
    blk_a = lambda g: jnp.clip(2 * g - 1, 0, last) % ne
    blk_b = lambda g: jnp.minimum(2 * g, last) % ne
    pair_mm1 = lambda g: (jnp.minimum(2 * g, last) % ne) // 2
    pair_v = lambda g: (jnp.maximum(2 * g - 2, 0) % ne) // 2
    feat = lambda rows, tok: pl.BlockSpec((rows, tm), lambda g: (0, tok(g)))
    nk = PEER_HEADS * N_KEYS
    t0_rows = (eb // N_KEYS) * PEER_HEADS
    return pl.pallas_call(
        functools.partial(_peer_dense_kernel, eb=eb, ne=ne, final_norm=final_norm),
        grid=(nblocks // 2 + 1,),
        in_specs=[
            feat(d_, tok_mm1),
            pl.BlockSpec((None, 2 * eb, d_), lambda g: (layer, pair_mm1(g), 0)),
            pl.BlockSpec((None, d_, 2 * eb), lambda g: (layer, 0, pair_v(g))),
            pl.BlockSpec((t0_rows, tm), lambda g: (blk_a(g), tok_a(g))), feat(nk, tok_a), feat(PEER_HEADS, tok_a),
            pl.BlockSpec((t0_rows, tm), lambda g: (blk_b(g), tok_mm1(g))), feat(nk, tok_mm1),
            feat(PEER_HEADS, tok_mm1),
            pl.BlockSpec((tm, d_), lambda g: (tok_v(g), 0)),
            pl.BlockSpec((1, d_), lambda g: (0, 0)),
        ],
        out_specs=pl.BlockSpec((tm, d_), lambda g: (tok_v(g), 0)),
        out_shape=jax.ShapeDtypeStruct((n_, d_), F32),
        scratch_shapes=[
            pltpu.VMEM((d_, tm), F32),
            pltpu.VMEM((eb, tm), F32), pltpu.VMEM((eb, tm), F32),
            pltpu.VMEM((eb, tm), BF16), pltpu.VMEM((eb, tm), BF16),
        ],
        compiler_params=_cparams("arbitrary"),
        name="peer_dense",
    )(xnt, w["u"], w["vt"], t0, t1, th, t0, t1, th, x1, gf)


def _rope(x, cos, sin_signed):
    half = HEAD_DIM // 2
    lane = lax.broadcasted_iota(jnp.int32, (1, V7X_LANES), 1)
    first = (lane % HEAD_DIM) < half
    outs = []
    for t in range(x.shape[1] // V7X_LANES):
        xt = x[:, t * V7X_LANES:(t + 1) * V7X_LANES]
        swapped = jnp.where(first, pltpu.roll(xt, V7X_LANES - half, axis=1), pltpu.roll(xt, half, axis=1))
        outs.append(xt * cos + swapped * sin_signed)
    return jnp.concatenate(outs, axis=1)


def _qkv_kernel(x_ref, gkv_ref, wkv_ref, gq_ref, wq_ref, cos_ref, sin_ref, q_ref, k_ref, v_ref):
    x = x_ref[...]
    cos = cos_ref[...]
    sin = sin_ref[...]
    kv = jnp.dot(_rms(x, gkv_ref[...]).astype(BF16), wkv_ref[...], preferred_element_type=F32)
    k_ref[...] = _rope(kv[:, :KV_DIM], cos, sin)
    v_ref[...] = kv[:, KV_DIM:]
    q = jnp.dot(_rms(x, gq_ref[...]).astype(BF16), wq_ref[...], preferred_element_type=F32)
    q_ref[...] = (_rope(q, cos, sin) * (HEAD_DIM ** -0.5)).astype(q_ref.dtype)


def _qkv(x, w, cos, sin, *, tm, table_blocks, q_dtype):
    n_, d_ = x.shape
    assert n_ % tm == 0
    const2 = lambda i: (0, 0)
    tok = lambda cols: pl.BlockSpec((tm, cols), lambda i: (i, 0))
    tab = pl.BlockSpec((tm, V7X_LANES), lambda i: (i % table_blocks, 0))
    return pl.pallas_call(
        _qkv_kernel,
        grid=(n_ // tm,),
        in_specs=[tok(d_), pl.BlockSpec((1, d_), const2), pl.BlockSpec((d_, 2 * KV_DIM), const2),
                  pl.BlockSpec((1, d_), const2), pl.BlockSpec((d_, Q_DIM), const2), tab, tab],
        out_specs=[tok(Q_DIM), tok(KV_DIM), tok(KV_DIM)],
        out_shape=[jax.ShapeDtypeStruct((n_, Q_DIM), q_dtype),
                   jax.ShapeDtypeStruct((n_, KV_DIM), F32),
                   jax.ShapeDtypeStruct((n_, KV_DIM), F32)],
        compiler_params=_cparams("arbitrary"),
        name="qkv_rope",
    )(x, w["g_kv"], w["w_kv"], w["g_q"], w["w_q"], cos, sin)


def _sink_attend(s, mask, sink, v):
    s = jnp.where(mask, s, -jnp.inf)
    m = jnp.maximum(jnp.max(s, axis=-1, keepdims=True), sink)
    p = jnp.exp(s - m)
    denom = jnp.sum(p, axis=-1, keepdims=True) + jnp.exp(sink - m)
    return jnp.dot(p.astype(BF16), v, preferred_element_type=F32) / denom


def _swa_prompt_kernel(sink_ref, q_ref, kp_ref, kc_ref, vp_ref, vc_ref, o_ref):
    n = pl.program_id(1)
    c = WINDOW
    k2 = jnp.concatenate([kp_ref[...], kc_ref[...]], axis=0).astype(BF16)
    v2 = jnp.concatenate([vp_ref[...], vc_ref[...]], axis=0).astype(BF16)
    qi = lax.broadcasted_iota(jnp.int32, (c, 2 * c), 0)
    kj = lax.broadcasted_iota(jnp.int32, (c, 2 * c), 1)
    mask = (kj > qi) & (kj <= qi + c) & ((kj >= c) | (n > 0))
    outs = []
    for hd in range(N_HEADS):
        kvh = hd // GROUP
        q = q_ref[:, hd * HEAD_DIM:(hd + 1) * HEAD_DIM]
        kk = k2[:, kvh * HEAD_DIM:(kvh + 1) * HEAD_DIM]
        s = lax.dot_general(q, kk, (((1,), (1,)), ((), ())), preferred_element_type=F32)
        outs.append(_sink_attend(s, mask, sink_ref[hd], v2[:, kvh * HEAD_DIM:(kvh + 1) * HEAD_DIM]))
    o_ref[...] = jnp.concatenate(outs, axis=1).astype(BF16)


def _swa_prompt(q, k, v, sinks, *, batch, seq):
    nb = seq // WINDOW
    cur = lambda cols: pl.BlockSpec((WINDOW, cols), lambda b, n: (b * nb + n, 0))
    prev = lambda cols: pl.BlockSpec((WINDOW, cols), lambda b, n: (b * nb + jnp.maximum(n - 1, 0), 0))
    return pl.pallas_call(
        _swa_prompt_kernel,
        grid=(batch, nb),
        in_specs=[pl.BlockSpec(memory_space=pltpu.SMEM),
                  cur(Q_DIM), prev(KV_DIM), cur(KV_DIM), prev(KV_DIM), cur(KV_DIM)],
        out_specs=cur(Q_DIM),
        out_shape=jax.ShapeDtypeStruct((batch * seq, Q_DIM), BF16),
        compiler_params=_cparams("arbitrary", "arbitrary"),
        name="swa_prompt",
    )(sinks, q, k, k, v, v)


def _swa_sample_kernel(sink_ref, q_ref, kc_ref, kn_ref, vc_ref, vn_ref, o_ref, ko_ref, vo_ref, *, nq, nb):
    c = WINDOW
    rows = GROUP * nq
    qi = lax.broadcasted_iota(jnp.int32, (rows, 2 * c), 0) % nq
    kj = lax.broadcasted_iota(jnp.int32, (rows, 2 * c), 1)
    mask = ((kj < c) & (kj > qi)) | ((kj >= c) & (kj - c <= qi))
    mask = mask[None]
    fill = jnp.zeros((nb, c - nq, KV_DIM), F32)
    k2 = jnp.concatenate([kc_ref[...], kn_ref[...], fill], axis=1).astype(BF16)
    v2 = jnp.concatenate([vc_ref[...], vn_ref[...], fill], axis=1).astype(BF16)
    q = q_ref[...]
    outs = [None] * N_HEADS
    for kvh in range(N_KV_HEADS):
        heads = range(kvh * GROUP, (kvh + 1) * GROUP)
        cols = slice(kvh * HEAD_DIM, (kvh + 1) * HEAD_DIM)
        qs = jnp.concatenate([q[:, :, hd * HEAD_DIM:(hd + 1) * HEAD_DIM] for hd in heads], axis=1).astype(BF16)
        sink = jnp.concatenate([jnp.full((1, nq, 1), sink_ref[hd], F32) for hd in heads], axis=1)
        s = jnp.einsum("bqd,bkd->bqk", qs, k2[:, :, cols], preferred_element_type=F32)
        s = jnp.where(mask, s, -jnp.inf)
        m = jnp.maximum(jnp.max(s, axis=-1, keepdims=True), sink)
        p = jnp.exp(s - m)
        denom = jnp.sum(p, axis=-1, keepdims=True) + jnp.exp(sink - m)
        o = jnp.einsum("bqk,bkd->bqd", p.astype(BF16), v2[:, :, cols], preferred_element_type=F32) / denom
        for g, hd in enumerate(heads):
            outs[hd] = o[:, g * nq:(g + 1) * nq, :]
    o_ref[...] = jnp.concatenate(outs, axis=2)
    for src_c, src_n, dst in ((kc_ref, kn_ref, ko_ref), (vc_ref, vn_ref, vo_ref)):
        dst[:, 0:c - nq, :] = src_c[:, nq:c, :]
        dst[:, c - nq:c, :] = src_n[...]


def _swa_sample(q, kc, kn, vc, vn, sinks):
    b_, nq, _ = q.shape
    assert nq <= WINDOW
    nb = _pick(b_, (8, 4, 2, 1))
    blk = lambda rows, cols: pl.BlockSpec((nb, rows, cols), lambda b: (b, 0, 0))
    return pl.pallas_call(
        functools.partial(_swa_sample_kernel, nq=nq, nb=nb),
        grid=(b_ // nb,),
        in_specs=[pl.BlockSpec(memory_space=pltpu.SMEM),
                  blk(nq, Q_DIM), blk(WINDOW, KV_DIM), blk(nq, KV_DIM), blk(WINDOW, KV_DIM), blk(nq, KV_DIM)],
        out_specs=[blk(nq, Q_DIM), blk(WINDOW, KV_DIM), blk(WINDOW, KV_DIM)],
        out_shape=[jax.ShapeDtypeStruct((b_, nq, Q_DIM), F32),
                   jax.ShapeDtypeStruct((b_, WINDOW, KV_DIM), F32),
                   jax.ShapeDtypeStruct((b_, WINDOW, KV_DIM), F32)],
        compiler_params=_cparams("arbitrary"),
        name="swa_sample",
    )(sinks, q, kc, kn, vc, vn)


def _rope_tables(pos):
    half = HEAD_DIM // 2
    inv = ROPE_THETA ** (-jnp.arange(half, dtype=F32) / half)
    ang = pos.astype(F32)[:, None] * inv[None, :]
    cos = jnp.cos(ang)
    sin = jnp.sin(ang)
    reps = V7X_LANES // HEAD_DIM
    cos_t = jnp.tile(jnp.concatenate([cos, cos], axis=1), (1, reps))
    sin_t = jnp.tile(jnp.concatenate([-sin, sin], axis=1), (1, reps))
    return cos_t, sin_t


def _pick(n, cands):
    for c in cands:
        if n % c == 0:
            return c
    raise ValueError(f"no tile for {n}")


def _peer(x1, pw, gf, attn=None, *, layer, final_norm):
    n_ = x1.shape[0]
    x1, front = _peer_front(x1, pw, attn, tm=_pick(n_, (512, 256, 128)))
    return _peer_dense(x1, front, pw, gf, layer=layer, tm=_pick(n_, (512, 256, 128)), eb=1024,
                       final_norm=final_norm)


def kernel(x_prompt, x_sample, state_conv, state_h, cache_k, cache_v, rg_norm, rg_w_in, rg_conv_w,
           rg_conv_b, rg_wa, rg_ba, rg_wx, rg_bx, rg_lambda, rg_w_out, kv_norm, w_kv, attn_norm, w_q,
           sinks, w_o, ffn_norm, peer_wq, peer_subkeys, peer_u, peer_v, final_norm):
    d = D_MODEL
    row = lambda a: a.reshape(1, -1).astype(F32)
    rgw = dict(g=row(rg_norm[0]), w_in=rg_w_in[0].astype(BF16), conv_w=rg_conv_w[0],
               conv_b=row(rg_conv_b[0]), wa=rg_wa[0].astype(BF16), ba=row(rg_ba[0]),
               wx=rg_wx[0].astype(BF16), bx=row(rg_bx[0]), lam=row(rg_lambda[0]),
               w_out=rg_w_out[0].astype(BF16))
    u_all = peer_u.astype(BF16)
    vt_all = jnp.swapaxes(peer_v.astype(BF16), 1, 2)
    peer_w = []
    for l in range(2):
        wq = peer_wq[l].reshape(d, PEER_HEADS, 2, D_KEY // 2).transpose(2, 1, 3, 0).reshape(PQ_DIM, d)
        skx0 = jnp.einsum("nc,hg->nhgc", peer_subkeys[l, 0], jnp.eye(PEER_HEADS, dtype=F32))
        skx0 = skx0.reshape(PEER_HEADS * N_KEYS, PQ_DIM // 2)
        peer_w.append(dict(g=row(ffn_norm[l]), wqt=wq.astype(BF16), skx0=skx0.astype(BF16),
                           sk1=peer_subkeys[l, 1].astype(BF16), u=u_all, vt=vt_all))
    attw = dict(g_kv=row(kv_norm), w_kv=w_kv.astype(BF16), g_q=row(attn_norm[0]), w_q=w_q[0].astype(BF16))
    wo = w_o[0].astype(BF16)
    sink = sinks[0].astype(F32)
    gf = row(final_norm)

    bp, tp, _ = x_prompt.shape
    steps_p = _pick(tp, (256, 128, 64, 32, 16, 8))
    pad_p = V7X_SUBLANES
    x1, conv_p, h_p = _rg_layer(x_prompt, jnp.zeros((bp, pad_p, d), F32), jnp.zeros((bp, 1, d), F32),
                                rgw, stride=1, steps=steps_p)
    xp = _peer(x1.reshape(bp * tp, d), peer_w[0], gf, layer=0, final_norm=False)
    cos_p, sin_p = _rope_tables(jnp.arange(tp))
    tm_p = _pick(tp, (512, 256, 128))
    q, k, v = _qkv(xp, attw, cos_p, sin_p, tm=tm_p, table_blocks=tp // tm_p, q_dtype=BF16)
    o = _swa_prompt(q, k, v, sink, batch=bp, seq=tp)
    y_prompt = _peer(xp, peer_w[1], gf, attn=(o, wo), layer=1, final_norm=True).reshape(bp, tp, d)
    k_p = k.reshape(bp, tp, N_KV_HEADS, HEAD_DIM)[:, -WINDOW:]
    v_p = v.reshape(bp, tp, N_KV_HEADS, HEAD_DIM)[:, -WINDOW:]
    conv_p = conv_p[None]
    h_p = h_p.reshape(1, bp, d)

    bs, ts, _ = x_sample.shape
    xs = x_sample.transpose(1, 0, 2).reshape(1, ts * bs, d)
    conv0 = state_conv[0].transpose(1, 0, 2).reshape(1, (CONV_W - 1) * bs, d)
    x1s, conv_s, h_s = _rg_layer(xs, conv0, state_h[0][None], rgw, stride=bs, steps=ts)
    xs1 = _peer(x1s.reshape(ts * bs, d), peer_w[0], gf, layer=0, final_norm=False)
    xs1 = xs1.reshape(ts, bs, d).transpose(1, 0, 2).reshape(bs * ts, d)
    cos_s, sin_s = _rope_tables(PAST_LEN + jnp.arange(ts))
    cos_s = jnp.tile(cos_s, (bs, 1))
    sin_s = jnp.tile(sin_s, (bs, 1))
    tm_s = _pick(bs * ts, (512, 256, 128, 64, 32, 16, 8))
    qs, ks, vs = _qkv(xs1, attw, cos_s, sin_s, tm=tm_s, table_blocks=(bs * ts) // tm_s, q_dtype=F32)
    ck = cache_k.reshape(bs, WINDOW, KV_DIM)
    cv = cache_v.reshape(bs, WINDOW, KV_DIM)
    kn = ks.reshape(bs, ts, KV_DIM)
    vn = vs.reshape(bs, ts, KV_DIM)
    os_, k_s, v_s = _swa_sample(qs.reshape(bs, ts, Q_DIM), ck, kn, cv, vn, sink)
    y_sample = _peer(xs1, peer_w[1], gf, attn=(os_.reshape(bs * ts, Q_DIM), wo), layer=1,
                     final_norm=True).reshape(bs, ts, d)
    k_s = k_s.reshape(bs, WINDOW, N_KV_HEADS, HEAD_DIM)
    v_s = v_s.reshape(bs, WINDOW, N_KV_HEADS, HEAD_DIM)
    conv_s = conv_s.reshape(CONV_W - 1, bs, d).transpose(1, 0, 2)[None]
    h_s = h_s.reshape(1, bs, d)

    return (y_prompt, y_sample, conv_p, h_p, k_p, v_p, conv_s, h_s, k_s, v_s)
```

```python
import functools
import math

import jax
import jax.numpy as jnp
from jax import lax
from jax.experimental import pallas as pl
from jax.experimental.pallas import tpu as pltpu

D_MODEL = 1024
PAST_LEN = 16384
D_RNN = D_MODEL
RG_BLOCKS = 8
RG_BLOCK_W = D_RNN // RG_BLOCKS
CONV_W = 4
RG_C = 8.0
HEAD_DIM = 64
N_HEADS = D_MODEL // HEAD_DIM
N_KV_HEADS = 4
GROUP = N_HEADS // N_KV_HEADS
WINDOW = 128
ROPE_THETA = 10000.0
N_KEYS = 128
N_EXPERTS = N_KEYS * N_KEYS
PEER_HEADS = 8
PEER_TOPK = 16
D_KEY = 256
EPS = 1e-6

KV_DIM = N_KV_HEADS * HEAD_DIM
Q_DIM = N_HEADS * HEAD_DIM
PQ_DIM = PEER_HEADS * D_KEY

V7X_LANES = 128
V7X_SUBLANES = 8
V7X_VMEM_LIMIT_BYTES = 56 * 1024 * 1024

BF16 = jnp.bfloat16
F32 = jnp.float32


def _cparams(*sem, flags=None):
    return pltpu.CompilerParams(dimension_semantics=sem, vmem_limit_bytes=V7X_VMEM_LIMIT_BYTES, flags=flags)


def _rms(x, g):
    return x * lax.rsqrt(jnp.mean(x * x, axis=-1, keepdims=True) + EPS) * g


def _gelu_tanh(x):
    u2 = (-2.0 * math.sqrt(2.0 / math.pi)) * (x * (1.0 + 0.044715 * (x * x)))
    return x / (1.0 + jnp.exp(u2))


def _sigmoid(x):
    return 1.0 / (1.0 + jnp.exp(-x))


def _rg_kernel(x_ref, conv0_ref, h0_ref, g_ref, win_ref, cw_ref, cb_ref, wa_ref, ba_ref,
               wx_ref, bx_ref, lam_ref, wout_ref,
               y_ref, conv_ref, hlast_ref,
               xbuf_ref, hcar_ref, *, stride, steps, pad):
    rows = stride * steps
    tail = (CONV_W - 1) * stride
    c = pl.program_id(1)

    @pl.when(c == 0)
    def _():
        xbuf_ref[0:pad, :] = conv0_ref[0]
        hcar_ref[...] = h0_ref[0]

    x = x_ref[0]
    xn = _rms(x, g_ref[...]).astype(BF16)
    proj = jnp.dot(xn, win_ref[...], preferred_element_type=F32)
    gate = _gelu_tanh(proj[:, :D_RNN])
    xr = proj[:, D_RNN:]
    xbuf_ref[pad:pad + rows, :] = xr

    y = cb_ref[...] + cw_ref[0:1, :] * xbuf_ref[pad - 3 * stride:pad - 3 * stride + rows, :]
    y = y + cw_ref[1:2, :] * xbuf_ref[pad - 2 * stride:pad - 2 * stride + rows, :]
    y = y + cw_ref[2:3, :] * xbuf_ref[pad - stride:pad - stride + rows, :]
    xc = y + cw_ref[3:4, :] * xr

    conv_ref[0] = xbuf_ref[pad + rows - tail:pad + rows, :]
    xbuf_ref[0:pad, :] = xbuf_ref[rows:rows + pad, :]

    xcb = xc.astype(BF16)
    rs, is_ = [], []
    for n in range(RG_BLOCKS):
        blk = xcb[:, n * RG_BLOCK_W:(n + 1) * RG_BLOCK_W]
        rs.append(jnp.dot(blk, wa_ref[n], preferred_element_type=F32))
        is_.append(jnp.dot(blk, wx_ref[n], preferred_element_type=F32))
    r = _sigmoid(jnp.concatenate(rs, axis=1) + ba_ref[...])
    i = _sigmoid(jnp.concatenate(is_, axis=1) + bx_ref[...])

    nlam = -lam_ref[...]
    softplus = jnp.maximum(nlam, 0.0) + jnp.log1p(jnp.exp(-jnp.abs(nlam)))
    log_a = -RG_C * r * softplus
    a = jnp.exp(log_a)
    mult = jnp.sqrt(-jnp.tanh(log_a) * (a * a + 1.0))
    b = mult * i * xc

    group = V7X_SUBLANES if (stride == 1 and steps % V7X_SUBLANES == 0) else steps
    row = lax.broadcasted_iota(jnp.int32, (rows, D_RNN), 0) % (group * stride)
    s = 1
    while s < group:
        sh = s * stride
        a_sh = pltpu.roll(a, sh, axis=0)
        b_sh = pltpu.roll(b, sh, axis=0)
        m = row >= sh
        b = jnp.where(m, a * b_sh + b, b)
        a = jnp.where(m, a * a_sh, a)
        s *= 2
    grows = group * stride
    carry = hcar_ref[...]
    hs = []
    for g0 in range(0, rows, grows):
        cg = carry if stride == 1 else jnp.concatenate([carry] * group, axis=0)
        hg_ = a[g0:g0 + grows, :] * cg + b[g0:g0 + grows, :]
        hs.append(hg_)
        carry = hg_[grows - stride:grows, :]
    h = hs[0] if len(hs) == 1 else jnp.concatenate(hs, axis=0)
    hl = carry
    hcar_ref[...] = hl
    hlast_ref[0] = hl

    hg = (h * gate).astype(BF16)
    y_ref[0] = x + jnp.dot(hg, wout_ref[...], preferred_element_type=F32)


def _rg_layer(x3, conv0, h0, w, *, stride, steps):
    g_, r_, d_ = x3.shape
    rows = stride * steps
    assert r_ % rows == 0
    pad = conv0.shape[1]
    tail = (CONV_W - 1) * stride
    nchunks = r_ // rows
    const2 = lambda g, c: (0, 0)
    const3 = lambda g, c: (0, 0, 0)
    kern = functools.partial(_rg_kernel, stride=stride, steps=steps, pad=pad)
    return pl.pallas_call(
        kern,
        grid=(g_, nchunks),
        in_specs=[
            pl.BlockSpec((1, rows, d_), lambda g, c: (g, c, 0)),
            pl.BlockSpec((1, pad, d_), lambda g, c: (g, 0, 0)),
            pl.BlockSpec((1, stride, d_), lambda g, c: (g, 0, 0)),
            pl.BlockSpec((1, d_), const2),
            pl.BlockSpec((d_, 2 * D_RNN), const2),
            pl.BlockSpec((CONV_W, D_RNN), const2),
            pl.BlockSpec((1, D_RNN), const2),
            pl.BlockSpec((RG_BLOCKS, RG_BLOCK_W, RG_BLOCK_W), const3),
            pl.BlockSpec((1, D_RNN), const2),
            pl.BlockSpec((RG_BLOCKS, RG_BLOCK_W, RG_BLOCK_W), const3),
            pl.BlockSpec((1, D_RNN), const2),
            pl.BlockSpec((1, D_RNN), const2),
            pl.BlockSpec((D_RNN, d_), const2),
        ],
        out_specs=[
            pl.BlockSpec((1, rows, d_), lambda g, c: (g, c, 0)),
            pl.BlockSpec((1, tail, D_RNN), lambda g, c: (g, 0, 0)),
            pl.BlockSpec((1, stride, D_RNN), lambda g, c: (g, 0, 0)),
        ],
        out_shape=[
            jax.ShapeDtypeStruct((g_, r_, d_), F32),
            jax.ShapeDtypeStruct((g_, tail, D_RNN), F32),
            jax.ShapeDtypeStruct((g_, stride, D_RNN), F32),
        ],
        scratch_shapes=[
            pltpu.VMEM((pad + rows, D_RNN), F32),
            pltpu.VMEM((stride, D_RNN), F32),
        ],
        compiler_params=_cparams("arbitrary", "arbitrary"),
        name="rg_layer",
    )(x3, conv0, h0, w["g"], w["w_in"], w["conv_w"], w["conv_b"], w["wa"], w["ba"],
      w["wx"], w["bx"], w["lam"], w["w_out"])


def _ce(v, i, j):
    hi = jnp.maximum(v[i], v[j])
    lo = jnp.minimum(v[i], v[j])
    v[i], v[j] = hi, lo


def _sort16(v):
    n = 16
    k = 2
    while k <= n:
        j = k // 2
        while j >= 1:
            for i in range(n):
                l = i ^ j
                if l > i:
                    if (i & k) == 0:
                        _ce(v, i, l)
                    else:
                        _ce(v, l, i)
            j //= 2
        k *= 2


def _merge_top16(a, b):
    v = [jnp.maximum(a[k], b[15 - k]) for k in range(16)]
    j = 8
    while j >= 1:
        for i in range(16):
            l = i ^ j
            if l > i:
                _ce(v, i, l)
        j //= 2
    return v


def _top16(vs):
    groups = []
    for g in range(len(vs) // 16):
        lst = list(vs[16 * g:16 * g + 16])
        _sort16(lst)
        groups.append(lst)
    while len(groups) > 1:
        groups = [_merge_top16(groups[i], groups[i + 1]) for i in range(0, len(groups), 2)]
    return groups[0]


def _kth_product(sv0, sv1, mul):
    cands = [mul(sv0[0], sv1[j]) for j in range(PEER_TOPK)]
    for i in range(1, PEER_TOPK):
        for j in range(PEER_TOPK // (i + 1)):
            cands.append(mul(sv0[i], sv1[j]))
    padded = list(cands)
    while len(padded) % 16:
        padded.append(jnp.full_like(cands[0], -1.0))
    return _top16(padded)[PEER_TOPK - 1], cands


def _peer_select(sc_ref, lt):
    nh = PEER_HEADS
    pack = lambda a, b: jnp.concatenate([a, b], axis=0).astype(BF16)
    sv0 = _top16([pack(sc_ref[0, lt, n * nh:(n + 1) * nh, :], sc_ref[0, lt + 1, n * nh:(n + 1) * nh, :])
                  for n in range(N_KEYS)])
    sv1 = _top16([pack(sc_ref[1, lt, pl.ds(n, nh, stride=N_KEYS), :], sc_ref[1, lt + 1, pl.ds(n, nh, stride=N_KEYS), :])
                  for n in range(N_KEYS)])
    mul = lambda a, b: a * b
    theta, cands = _kth_product(sv0, sv1, mul)
    z = jnp.zeros(theta.shape, F32)
    for cnd in cands:
        z = z + jnp.where(cnd >= theta, cnd, jnp.zeros_like(cnd)).astype(F32)
    scale = 0.5 / z
    sv0n = [(v.astype(F32) * scale).astype(BF16) for v in sv0]
    theta_n, _ = _kth_product(sv0n, sv1, mul)
    return scale, theta_n.astype(F32)


def _peer_front_kernel(*refs, tm, with_attn):
    if with_attn:
        (x_ref, o_ref, wo_ref, g_ref, wqt_ref, skx_ref, sk_ref,
         x1_ref, xnt_ref, t0_ref, t1_ref, th_ref, sc_ref) = refs
        x1 = x_ref[...] + jnp.dot(o_ref[...].astype(BF16), wo_ref[...], preferred_element_type=F32)
        x1_ref[...] = x1
    else:
        (x_ref, g_ref, wqt_ref, skx_ref, sk_ref, xnt_ref, t0_ref, t1_ref, th_ref, sc_ref) = refs
        x1 = x_ref[...]
    nh = PEER_HEADS
    xn = _rms(x1, g_ref[...])
    xnt = xn.T.astype(BF16)
    xnt_ref[...] = xnt
    qt = jnp.dot(wqt_ref[...], xnt, preferred_element_type=F32).astype(BF16)
    s0 = jnp.dot(skx_ref[...], qt[0:PQ_DIM // 2, :], preferred_element_type=F32)
    mx = s0[0:nh, :]
    for n in range(1, N_KEYS):
        mx = jnp.maximum(mx, s0[n * nh:(n + 1) * nh, :])
    for n in range(N_KEYS):
        t = jnp.exp(s0[n * nh:(n + 1) * nh, :] - mx)
        for lt in range(tm // V7X_LANES):
            sc_ref[0, lt, n * nh:(n + 1) * nh, :] = t[:, lt * V7X_LANES:(lt + 1) * V7X_LANES]
    for h in range(nh):
        base = PQ_DIM // 2 + h * (D_KEY // 2)
        s = jnp.dot(sk_ref[...], qt[base:base + D_KEY // 2, :], preferred_element_type=F32)
        t = jnp.exp(s - jnp.max(s, axis=0, keepdims=True)).astype(BF16)
        t1_ref[h * N_KEYS:(h + 1) * N_KEYS, :] = t
        t = t.astype(F32)
        for lt in range(tm // V7X_LANES):
            sc_ref[1, lt, h * N_KEYS:(h + 1) * N_KEYS, :] = t[:, lt * V7X_LANES:(lt + 1) * V7X_LANES]
    for lt in range(0, tm // V7X_LANES, 2):
        scale, theta = _peer_select(sc_ref, lt)
        for j in range(2):
            lanes = pl.ds((lt + j) * V7X_LANES, V7X_LANES)
            th_ref[:, lanes] = theta[j * nh:(j + 1) * nh, :]
            sc = scale[j * nh:(j + 1) * nh, :]
            for n in range(N_KEYS):
                t0b = sc_ref[0, lt + j, n * nh:(n + 1) * nh, :].astype(BF16).astype(F32)
                t0_ref[n * nh:(n + 1) * nh, lanes] = (t0b * sc).astype(BF16).astype(F32)


def _peer_front(x, w, attn=None, *, tm):
    n_, d_ = x.shape
    assert n_ % tm == 0 and tm % (2 * V7X_LANES) == 0
    with_attn = attn is not None
    const2 = lambda i: (0, 0)
    tok = pl.BlockSpec((tm, d_), lambda i: (i, 0))
    in_specs = [tok]
    args = [x]
    if with_attn:
        in_specs += [pl.BlockSpec((tm, Q_DIM), lambda i: (i, 0)), pl.BlockSpec((Q_DIM, d_), const2)]
        args += list(attn)
    in_specs += [
        pl.BlockSpec((1, d_), const2),
        pl.BlockSpec((PQ_DIM, d_), const2),
        pl.BlockSpec((PEER_HEADS * N_KEYS, PQ_DIM // 2), const2),
        pl.BlockSpec((N_KEYS, D_KEY // 2), const2),
    ]
    args += [w["g"], w["wqt"], w["skx0"], w["sk1"]]
    feat = lambda rows: pl.BlockSpec((rows, tm), lambda i: (0, i))
    out_specs = [feat(d_), feat(PEER_HEADS * N_KEYS), feat(PEER_HEADS * N_KEYS), feat(PEER_HEADS)]
    out_shape = [
        jax.ShapeDtypeStruct((d_, n_), BF16),
        jax.ShapeDtypeStruct((PEER_HEADS * N_KEYS, n_), F32),
        jax.ShapeDtypeStruct((PEER_HEADS * N_KEYS, n_), BF16),
        jax.ShapeDtypeStruct((PEER_HEADS, n_), F32),
    ]
    if with_attn:
        out_specs = [tok] + out_specs
        out_shape = [jax.ShapeDtypeStruct((n_, d_), F32)] + out_shape
    outs = pl.pallas_call(
        functools.partial(_peer_front_kernel, tm=tm, with_attn=with_attn),
        grid=(n_ // tm,),
        in_specs=in_specs,
        out_specs=out_specs,
        out_shape=out_shape,
        scratch_shapes=[pltpu.VMEM((2, tm // V7X_LANES, PEER_HEADS * N_KEYS, V7X_LANES), F32)],
        compiler_params=_cparams("arbitrary"),
        name="peer_front_attn" if with_attn else "peer_front",
    )(*args)
    if with_attn:
        return outs[0], outs[1:]
    return x, outs


def _peer_weight_tile(lt, keys, ht_ref, pt_ref, t0_ref, t1_ref, th_ref):
    rb = 2 * V7X_SUBLANES
    lanes = slice(lt * V7X_LANES, (lt + 1) * V7X_LANES)
    bcast = lambda row: jnp.broadcast_to(row, (rb, V7X_LANES)).astype(BF16)
    th = [bcast(th_ref[h:h + 1, lanes]) for h in range(PEER_HEADS)]
    zero = jnp.zeros((rb, V7X_LANES), BF16)
    for al in keys:
        ta = [bcast(t0_ref[al * PEER_HEADS + h:al * PEER_HEADS + h + 1, lanes]) for h in range(PEER_HEADS)]
        for bg in range(N_KEYS // rb):
            w = None
            for h in range(PEER_HEADS):
                prod = ta[h] * t1_ref[h * N_KEYS + bg * rb:h * N_KEYS + (bg + 1) * rb, lanes]
                wh = jnp.where(prod >= th[h], prod, zero)
                w = wh if w is None else w + wh
            r0 = al * N_KEYS + bg * rb
            hv = ht_ref[r0:r0 + rb, lanes]
            gl = hv * (1.0 + lax.erf(hv * (1.0 / math.sqrt(2.0))))
            pt_ref[r0:r0 + rb, lanes] = gl.astype(BF16) * w


def _peer_half_step(xnt_ref, u_ref, u_rows, vt_ref, vt_cols, acc_ref, ht_w, pt_r, ht_r, pt_w,
                    t0_ref, t1_ref, th_ref, *, eb, tm):
    mxu_w = 2 * V7X_LANES
    pieces = []
    for c in range(tm // mxu_w):
        cols = slice(c * mxu_w, (c + 1) * mxu_w)

        def scores(cols=cols):
            ht_w[:, cols] = jnp.dot(u_ref[u_rows, :], xnt_ref[:, cols], preferred_element_type=F32)

        def values(cols=cols):
            acc_ref[:, cols] += jnp.dot(vt_ref[:, vt_cols], pt_r[:, cols], preferred_element_type=F32)

        pieces += [scores, values]
    for lt in range(tm // V7X_LANES):
        _peer_weight_tile(lt, range(eb // N_KEYS), ht_r, pt_w, t0_ref, t1_ref, th_ref)
        pieces[lt]()


def _peer_dense_kernel(xnt_ref, u_ref, vt_ref, t0a_ref, t1a_ref, tha_ref, t0b_ref, t1b_ref, thb_ref,
                       x1_ref, gf_ref, out_ref, acc_ref, ht0_ref, ht1_ref, pt0_ref, pt1_ref,
                       *, eb, ne, final_norm):
    g = pl.program_id(0)
    tm = acc_ref.shape[1]
    fv = 2 * g - 2

    @pl.when(g == 0)
    def _():
        ht1_ref[...] = jnp.zeros_like(ht1_ref)
        pt0_ref[...] = jnp.zeros_like(pt0_ref)

    @pl.when((g == 0) | (lax.rem(fv, ne) == 0))
    def _():
        acc_ref[...] = jnp.zeros_like(acc_ref)

    _peer_half_step(xnt_ref, u_ref, slice(0, eb), vt_ref, slice(0, eb), acc_ref,
                    ht0_ref, pt0_ref, ht1_ref, pt1_ref, t0a_ref, t1a_ref, tha_ref, eb=eb, tm=tm)
    _peer_half_step(xnt_ref, u_ref, slice(eb, 2 * eb), vt_ref, slice(eb, 2 * eb), acc_ref,
                    ht1_ref, pt1_ref, ht0_ref, pt0_ref, t0b_ref, t1b_ref, thb_ref, eb=eb, tm=tm)

    @pl.when((g > 0) & (lax.rem(fv + 1, ne) == ne - 1))
    def _():
        y = x1_ref[...] + acc_ref[...].T
        if final_norm:
            y = _rms(y, gf_ref[...])
        out_ref[...] = y


def _peer_dense(x1, front, w, gf, *, layer, tm, eb, final_norm):
    xnt, t0, t1, th = front
    n_, d_ = x1.shape
    ne = N_EXPERTS // eb
    assert n_ % tm == 0 and N_EXPERTS % eb == 0 and eb % N_KEYS == 0 and ne % 2 == 0
    assert tm % (2 * V7X_LANES) == 0
    nblocks = (n_ // tm) * ne
    last = nblocks - 1
    tok_mm1 = lambda g: jnp.minimum(2 * g, last) // ne
    tok_a = lambda g: jnp.clip(2 * g - 1, 0, last) // ne
    tok_v = lambda g: jnp.maximum(2 * g - 2, 0) // ne
    blk_a = lambda g: jnp.clip(2 * g - 1, 0, last) % ne
    blk_b = lambda g: jnp.minimum(2 * g, last) % ne
    pair_mm1 = lambda g: (jnp.minimum(2 * g, last) % ne) // 2
    pair_v = lambda g: (jnp.maximum(2 * g - 2, 0) % ne) // 2
    feat = lambda rows, tok: pl.BlockSpec((rows, tm), lambda g: (0, tok(g)))
    nk = PEER_HEADS * N_KEYS
    t0_rows = (eb // N_KEYS) * PEER_HEADS
    return pl.pallas_call(
        functools.partial(_peer_dense_kernel, eb=eb, ne=ne, final_norm=final_norm),
        grid=(nblocks // 2 + 1,),
        in_specs=[
            feat(d_, tok_mm1),
            pl.BlockSpec((None, 2 * eb, d_), lambda g: (layer, pair_mm1(g), 0)),
            pl.BlockSpec((None, d_, 2 * eb), lambda g: (layer, 0, pair_v(g))),
            pl.BlockSpec((t0_rows, tm), lambda g: (blk_a(g), tok_a(g))), feat(nk, tok_a), feat(PEER_HEADS, tok_a),
            pl.BlockSpec((t0_rows, tm), lambda g: (blk_b(g), tok_mm1(g))), feat(nk, tok_mm1),
            feat(PEER_HEADS, tok_mm1),
            pl.BlockSpec((tm, d_), lambda g: (tok_v(g), 0)),
            pl.BlockSpec((1, d_), lambda g: (0, 0)),
        ],
        out_specs=pl.BlockSpec((tm, d_), lambda g: (tok_v(g), 0)),
        out_shape=jax.ShapeDtypeStruct((n_, d_), F32),
        scratch_shapes=[
            pltpu.VMEM((d_, tm), F32),
            pltpu.VMEM((eb, tm), F32), pltpu.VMEM((eb, tm), F32),
            pltpu.VMEM((eb, tm), BF16), pltpu.VMEM((eb, tm), BF16),
        ],
        compiler_params=_cparams("arbitrary"),
        name="peer_dense",
    )(xnt, w["u"], w["vt"], t0, t1, th, t0, t1, th, x1, gf)


def _rope(x, cos, sin_signed):
    half = HEAD_DIM // 2
    lane = lax.broadcasted_iota(jnp.int32, (1, V7X_LANES), 1)
    first = (lane % HEAD_DIM) < half
    outs = []
    for t in range(x.shape[1] // V7X_LANES):
        xt = x[:, t * V7X_LANES:(t + 1) * V7X_LANES]
        swapped = jnp.where(first, pltpu.roll(xt, V7X_LANES - half, axis=1), pltpu.roll(xt, half, axis=1))
        outs.append(xt * cos + swapped * sin_signed)
    return jnp.concatenate(outs, axis=1)


def _qkv_kernel(x_ref, gkv_ref, wkv_ref, gq_ref, wq_ref, cos_ref, sin_ref, q_ref, k_ref, v_ref):
    x = x_ref[...]
    cos = cos_ref[...]
    sin = sin_ref[...]
    kv = jnp.dot(_rms(x, gkv_ref[...]).astype(BF16), wkv_ref[...], preferred_element_type=F32)
    k_ref[...] = _rope(kv[:, :KV_DIM], cos, sin)
    v_ref[...] = kv[:, KV_DIM:]
    q = jnp.dot(_rms(x, gq_ref[...]).astype(BF16), wq_ref[...], preferred_element_type=F32)
    q_ref[...] = (_rope(q, cos, sin) * (HEAD_DIM ** -0.5)).astype(q_ref.dtype)


def _qkv(x, w, cos, sin, *, tm, table_blocks, q_dtype):
    n_, d_ = x.shape
    assert n_ % tm == 0
    const2 = lambda i: (0, 0)
    tok = lambda cols: pl.BlockSpec((tm, cols), lambda i: (i, 0))
    tab = pl.BlockSpec((tm, V7X_LANES), lambda i: (i % table_blocks, 0))
    return pl.pallas_call(
        _qkv_kernel,
        grid=(n_ // tm,),
        in_specs=[tok(d_), pl.BlockSpec((1, d_), const2), pl.BlockSpec((d_, 2 * KV_DIM), const2),
                  pl.BlockSpec((1, d_), const2), pl.BlockSpec((d_, Q_DIM), const2), tab, tab],
        out_specs=[tok(Q_DIM), tok(KV_DIM), tok(KV_DIM)],
        out_shape=[jax.ShapeDtypeStruct((n_, Q_DIM), q_dtype),
                   jax.ShapeDtypeStruct((n_, KV_DIM), F32),
                   jax.ShapeDtypeStruct((n_, KV_DIM), F32)],
        compiler_params=_cparams("arbitrary"),
        name="qkv_rope",
    )(x, w["g_kv"], w["w_kv"], w["g_q"], w["w_q"], cos, sin)


def _sink_attend(s, mask, sink, v):
    s = jnp.where(mask, s, -jnp.inf)
    m = jnp.maximum(jnp.max(s, axis=-1, keepdims=True), sink)
    p = jnp.exp(s - m)
    denom = jnp.sum(p, axis=-1, keepdims=True) + jnp.exp(sink - m)
    return jnp.dot(p.astype(BF16), v, preferred_element_type=F32) / denom


def _swa_prompt_kernel(sink_ref, q_ref, kp_ref, kc_ref, vp_ref, vc_ref, o_ref):
    n = pl.program_id(1)
    c = WINDOW
    k2 = jnp.concatenate([kp_ref[...], kc_ref[...]], axis=0).astype(BF16)
    v2 = jnp.concatenate([vp_ref[...], vc_ref[...]], axis=0).astype(BF16)
    qi = lax.broadcasted_iota(jnp.int32, (c, 2 * c), 0)
    kj = lax.broadcasted_iota(jnp.int32, (c, 2 * c), 1)
    mask = (kj > qi) & (kj <= qi + c) & ((kj >= c) | (n > 0))
    outs = []
    for hd in range(N_HEADS):
        kvh = hd // GROUP
        q = q_ref[:, hd * HEAD_DIM:(hd + 1) * HEAD_DIM]
        kk = k2[:, kvh * HEAD_DIM:(kvh + 1) * HEAD_DIM]
        s = lax.dot_general(q, kk, (((1,), (1,)), ((), ())), preferred_element_type=F32)
        outs.append(_sink_attend(s, mask, sink_ref[hd], v2[:, kvh * HEAD_DIM:(kvh + 1) * HEAD_DIM]))
    o_ref[...] = jnp.concatenate(outs, axis=1).astype(BF16)


def _swa_prompt(q, k, v, sinks, *, batch, seq):
    nb = seq // WINDOW
    cur = lambda cols: pl.BlockSpec((WINDOW, cols), lambda b, n: (b * nb + n, 0))
    prev = lambda cols: pl.BlockSpec((WINDOW, cols), lambda b, n: (b * nb + jnp.maximum(n - 1, 0), 0))
    return pl.pallas_call(
        _swa_prompt_kernel,
        grid=(batch, nb),
        in_specs=[pl.BlockSpec(memory_space=pltpu.SMEM),
                  cur(Q_DIM), prev(KV_DIM), cur(KV_DIM), prev(KV_DIM), cur(KV_DIM)],
        out_specs=cur(Q_DIM),
        out_shape=jax.ShapeDtypeStruct((batch * seq, Q_DIM), BF16),
        compiler_params=_cparams("arbitrary", "arbitrary"),
        name="swa_prompt",
    )(sinks, q, k, k, v, v)


def _swa_sample_kernel(sink_ref, q_ref, kc_ref, kn_ref, vc_ref, vn_ref, o_ref, ko_ref, vo_ref, *, nq, nb):
    c = WINDOW
    rows = GROUP * nq
    qi = lax.broadcasted_iota(jnp.int32, (rows, 2 * c), 0) % nq
    kj = lax.broadcasted_iota(jnp.int32, (rows, 2 * c), 1)
    mask = ((kj < c) & (kj > qi)) | ((kj >= c) & (kj - c <= qi))
    mask = mask[None]
    fill = jnp.zeros((nb, c - nq, KV_DIM), F32)
    k2 = jnp.concatenate([kc_ref[...], kn_ref[...], fill], axis=1).astype(BF16)
    v2 = jnp.concatenate([vc_ref[...], vn_ref[...], fill], axis=1).astype(BF16)
    q = q_ref[...]
    outs = [None] * N_HEADS
    for kvh in range(N_KV_HEADS):
        heads = range(kvh * GROUP, (kvh + 1) * GROUP)
        cols = slice(kvh * HEAD_DIM, (kvh + 1) * HEAD_DIM)
        qs = jnp.concatenate([q[:, :, hd * HEAD_DIM:(hd + 1) * HEAD_DIM] for hd in heads], axis=1).astype(BF16)
        sink = jnp.concatenate([jnp.full((1, nq, 1), sink_ref[hd], F32) for hd in heads], axis=1)
        s = jnp.einsum("bqd,bkd->bqk", qs, k2[:, :, cols], preferred_element_type=F32)
        s = jnp.where(mask, s, -jnp.inf)
        m = jnp.maximum(jnp.max(s, axis=-1, keepdims=True), sink)
        p = jnp.exp(s - m)
        denom = jnp.sum(p, axis=-1, keepdims=True) + jnp.exp(sink - m)
        o = jnp.einsum("bqk,bkd->bqd", p.astype(BF16), v2[:, :, cols], preferred_element_type=F32) / denom
        for g, hd in enumerate(heads):
            outs[hd] = o[:, g * nq:(g + 1) * nq, :]
    o_ref[...] = jnp.concatenate(outs, axis=2)
    for src_c, src_n, dst in ((kc_ref, kn_ref, ko_ref), (vc_ref, vn_ref, vo_ref)):
        dst[:, 0:c - nq, :] = src_c[:, nq:c, :]
        dst[:, c - nq:c, :] = src_n[...]


def _swa_sample(q, kc, kn, vc, vn, sinks):
    b_, nq, _ = q.shape
    assert nq <= WINDOW
    nb = _pick(b_, (8, 4, 2, 1))
    blk = lambda rows, cols: pl.BlockSpec((nb, rows, cols), lambda b: (b, 0, 0))
    return pl.pallas_call(
        functools.partial(_swa_sample_kernel, nq=nq, nb=nb),
        grid=(b_ // nb,),
        in_specs=[pl.BlockSpec(memory_space=pltpu.SMEM),
                  blk(nq, Q_DIM), blk(WINDOW, KV_DIM), blk(nq, KV_DIM), blk(WINDOW, KV_DIM), blk(nq, KV_DIM)],
        out_specs=[blk(nq, Q_DIM), blk(WINDOW, KV_DIM), blk(WINDOW, KV_DIM)],
        out_shape=[jax.ShapeDtypeStruct((b_, nq, Q_DIM), F32),
                   jax.ShapeDtypeStruct((b_, WINDOW, KV_DIM), F32),
                   jax.ShapeDtypeStruct((b_, WINDOW, KV_DIM), F32)],
        compiler_params=_cparams("arbitrary"),
        name="swa_sample",
    )(sinks, q, kc, kn, vc, vn)


def _rope_tables(pos):
    half = HEAD_DIM // 2
    inv = ROPE_THETA ** (-jnp.arange(half, dtype=F32) / half)
    ang = pos.astype(F32)[:, None] * inv[None, :]
    cos = jnp.cos(ang)
    sin = jnp.sin(ang)
    reps = V7X_LANES // HEAD_DIM
    cos_t = jnp.tile(jnp.concatenate([cos, cos], axis=1), (1, reps))
    sin_t = jnp.tile(jnp.concatenate([-sin, sin], axis=1), (1, reps))
    return cos_t, sin_t


def _pick(n, cands):
    for c in cands:
        if n % c == 0:
            return c
    raise ValueError(f"no tile for {n}")


def _peer(x1, pw, gf, attn=None, *, layer, final_norm):
    n_ = x1.shape[0]
    x1, front = _peer_front(x1, pw, attn, tm=_pick(n_, (512, 256, 128)))
    return _peer_dense(x1, front, pw, gf, layer=layer, tm=_pick(n_, (512, 256, 128)), eb=1024,
                       final_norm=final_norm)


def kernel(x_prompt, x_sample, state_conv, state_h, cache_k, cache_v, rg_norm, rg_w_in, rg_conv_w,
           rg_conv_b, rg_wa, rg_ba, rg_wx, rg_bx, rg_lambda, rg_w_out, kv_norm, w_kv, attn_norm, w_q,
           sinks, w_o, ffn_norm, peer_wq, peer_subkeys, peer_u, peer_v, final_norm):
    d = D_MODEL
    row = lambda a: a.reshape(1, -1).astype(F32)
    rgw = dict(g=row(rg_norm[0]), w_in=rg_w_in[0].astype(BF16), conv_w=rg_conv_w[0],
               conv_b=row(rg_conv_b[0]), wa=rg_wa[0].astype(BF16), ba=row(rg_ba[0]),
               wx=rg_wx[0].astype(BF16), bx=row(rg_bx[0]), lam=row(rg_lambda[0]),
               w_out=rg_w_out[0].astype(BF16))
    u_all = peer_u.astype(BF16)
    vt_all = jnp.swapaxes(peer_v.astype(BF16), 1, 2)
    peer_w = []
    for l in range(2):
        wq = peer_wq[l].reshape(d, PEER_HEADS, 2, D_KEY // 2).transpose(2, 1, 3, 0).reshape(PQ_DIM, d)
        skx0 = jnp.einsum("nc,hg->nhgc", peer_subkeys[l, 0], jnp.eye(PEER_HEADS, dtype=F32))
        skx0 = skx0.reshape(PEER_HEADS * N_KEYS, PQ_DIM // 2)
        peer_w.append(dict(g=row(ffn_norm[l]), wqt=wq.astype(BF16), skx0=skx0.astype(BF16),
                           sk1=peer_subkeys[l, 1].astype(BF16), u=u_all, vt=vt_all))
    attw = dict(g_kv=row(kv_norm), w_kv=w_kv.astype(BF16), g_q=row(attn_norm[0]), w_q=w_q[0].astype(BF16))
    wo = w_o[0].astype(BF16)
    sink = sinks[0].astype(F32)
    gf = row(final_norm)

    bp, tp, _ = x_prompt.shape
    steps_p = _pick(tp, (256, 128, 64, 32, 16, 8))
    pad_p = V7X_SUBLANES
    x1, conv_p, h_p = _rg_layer(x_prompt, jnp.zeros((bp, pad_p, d), F32), jnp.zeros((bp, 1, d), F32),
                                rgw, stride=1, steps=steps_p)
    xp = _peer(x1.reshape(bp * tp, d), peer_w[0], gf, layer=0, final_norm=False)
    cos_p, sin_p = _rope_tables(jnp.arange(tp))
    tm_p = _pick(tp, (512, 256, 128))
    q, k, v = _qkv(xp, attw, cos_p, sin_p, tm=tm_p, table_blocks=tp // tm_p, q_dtype=BF16)
    o = _swa_prompt(q, k, v, sink, batch=bp, seq=tp)
    y_prompt = _peer(xp, peer_w[1], gf, attn=(o, wo), layer=1, final_norm=True).reshape(bp, tp, d)
    k_p = k.reshape(bp, tp, N_KV_HEADS, HEAD_DIM)[:, -WINDOW:]
    v_p = v.reshape(bp, tp, N_KV_HEADS, HEAD_DIM)[:, -WINDOW:]
    conv_p = conv_p[None]
    h_p = h_p.reshape(1, bp, d)

    bs, ts, _ = x_sample.shape
    xs = x_sample.transpose(1, 0, 2).reshape(1, ts * bs, d)
    conv0 = state_conv[0].transpose(1, 0, 2).reshape(1, (CONV_W - 1) * bs, d)
    x1s, conv_s, h_s = _rg_layer(xs, conv0, state_h[0][None], rgw, stride=bs, steps=ts)
    xs1 = _peer(x1s.reshape(ts * bs, d), peer_w[0], gf, layer=0, final_norm=False)
    xs1 = xs1.reshape(ts, bs, d).transpose(1, 0, 2).reshape(bs * ts, d)
    cos_s, sin_s = _rope_tables(PAST_LEN + jnp.arange(ts))
    cos_s = jnp.tile(cos_s, (bs, 1))
    sin_s = jnp.tile(sin_s, (bs, 1))
    tm_s = _pick(bs * ts, (512, 256, 128, 64, 32, 16, 8))
    qs, ks, vs = _qkv(xs1, attw, cos_s, sin_s, tm=tm_s, table_blocks=(bs * ts) // tm_s, q_dtype=F32)
    ck = cache_k.reshape(bs, WINDOW, KV_DIM)
    cv = cache_v.reshape(bs, WINDOW, KV_DIM)
    kn = ks.reshape(bs, ts, KV_DIM)
    vn = vs.reshape(bs, ts, KV_DIM)
    os_, k_s, v_s = _swa_sample(qs.reshape(bs, ts, Q_DIM), ck, kn, cv, vn, sink)
    y_sample = _peer(xs1, peer_w[1], gf, attn=(os_.reshape(bs * ts, Q_DIM), wo), layer=1,
                     final_norm=True).reshape(bs, ts, d)
    k_s = k_s.reshape(bs, WINDOW, N_KV_HEADS, HEAD_DIM)
    v_s = v_s.reshape(bs, WINDOW, N_KV_HEADS, HEAD_DIM)
    conv_s = conv_s.reshape(CONV_W - 1, bs, d).transpose(1, 0, 2)[None]
    h_s = h_s.reshape(1, bs, d)

    return (y_prompt, y_sample, conv_p, h_p, k_p, v_p, conv_s, h_s, k_s, v_s)
```
